```python
import jax, jax.numpy as jnp
from jax import lax
import numpy as np

D_MODEL = 1024
BATCH = 8
SEQ = 2048
DEPTH = 1
DEC_BATCH = 128
DEC_SEQ = 8
PAST_LEN = 8192
PAGE_SIZE = 128

N_META = 16
N_HEADS = 8
HEAD_DIM = 64
ATT_WIDTH = N_HEADS * HEAD_DIM
CONV_CH = D_MODEL // 2
CONV_WIDTH = 31
N_EXPERTS = 32
TOP_K = 4
D_EXPERT = D_MODEL
SWIGLU_LIMIT = 7.0
SWIGLU_ALPHA = 1.702
Q_BLOCK = 128
MOE_BLOCK = 128
NORM_EPS = 1e-5
SCALE = HEAD_DIM ** -0.5
FORGET_BIAS = 3.0
SPLITS = [ATT_WIDTH, 2 * ATT_WIDTH, 3 * ATT_WIDTH, 3 * ATT_WIDTH + N_HEADS,
          3 * ATT_WIDTH + N_HEADS + 2 * CONV_CH]
D_IN = 3 * ATT_WIDTH + N_HEADS + 2 * CONV_CH + 2 * D_MODEL

kernel_name = 'fox_conformer_moe_hybrid_step'


def rms_norm(x, g):
    xf = x.astype(jnp.float32)
    y = xf * lax.rsqrt(jnp.mean(xf * xf, axis=-1, keepdims=True) + NORM_EPS)
    return (y * g.astype(jnp.float32)).astype(x.dtype)


def in_project(h, w_in, b_forget):
    z = h @ w_in
    q, k, v, f, u2, g = jnp.split(z, SPLITS, axis=-1)
    shp = h.shape[:-1] + (N_HEADS, HEAD_DIM)
    logf = jax.nn.log_sigmoid(f.astype(jnp.float32) + b_forget.astype(jnp.float32))
    u = u2[..., :CONV_CH] * jax.nn.sigmoid(u2[..., CONV_CH:])
    return q.reshape(shp), k.reshape(shp), v.reshape(shp), logf, u, g


def conv_branch(u_ctx, w_dw, b_dw, ln_g, ln_b, w_pw2):
    y = lax.conv_general_dilated(u_ctx, w_dw[:, None, :].astype(u_ctx.dtype), window_strides=(1,),
                                 padding='VALID', dimension_numbers=('NWC', 'WIO', 'NWC'),
                                 feature_group_count=CONV_CH) + b_dw
    yf = y.astype(jnp.float32)
    mu = jnp.mean(yf, axis=-1, keepdims=True)
    var = jnp.mean(jnp.square(yf - mu), axis=-1, keepdims=True)
    yn = (yf - mu) * lax.rsqrt(var + NORM_EPS) * ln_g.astype(jnp.float32) + ln_b.astype(jnp.float32)
    return jax.nn.silu(yn).astype(u_ctx.dtype) @ w_pw2


def fox_prompt(q, k, v, logf):
    B, L = q.shape[:2]
    c = jnp.cumsum(logf, axis=1)
    ck = jnp.swapaxes(c, 1, 2)[:, :, None, :]
    pos = jnp.arange(L)

    def block(qb, cb, pb):
        s = jnp.einsum('bqhd,bkhd->bhqk', qb, k, preferred_element_type=jnp.float32) * SCALE
        s = s + jnp.swapaxes(cb, 1, 2)[..., :, None] - ck
        s = jnp.where(pos[None, :] <= pb[:, None], s, -jnp.inf)
        p = jax.nn.softmax(s, axis=-1)
        return jnp.einsum('bhqk,bkhd->bqhd', p.astype(v.dtype), v)

    o_meta = block(q[:, :N_META], c[:, :N_META], pos[:N_META])
    nb = (L - N_META) // Q_BLOCK
    qr = q[:, N_META:].reshape(B, nb, Q_BLOCK, N_HEADS, HEAD_DIM).transpose(1, 0, 2, 3, 4)
    cr = c[:, N_META:].reshape(B, nb, Q_BLOCK, N_HEADS).transpose(1, 0, 2, 3)
    pr = pos[N_META:].reshape(nb, Q_BLOCK)
    o_real = lax.map(lambda a: block(a[0], a[1], a[2]), (qr, cr, pr))
    o_real = o_real.transpose(1, 0, 2, 3, 4).reshape(B, L - N_META, N_HEADS, HEAD_DIM)
    return jnp.concatenate([o_meta, o_real], axis=1).reshape(B, L, ATT_WIDTH)


def fox_sample(q, k_new, v_new, logf_new, cache_k, cache_v, cache_logf, page_table, layer):
    DB, T = q.shape[:2]
    n_pages = page_table.shape[1]
    lf_past = cache_logf[layer, page_table].astype(jnp.float32).reshape(DB, n_pages * PAGE_SIZE, N_HEADS)
    c_past = jnp.cumsum(lf_past, axis=1)
    c_new = c_past[:, -1:] + jnp.cumsum(logf_new, axis=1)
    cq = jnp.swapaxes(c_new, 1, 2)[..., None]
    c_pages = c_past.reshape(DB, n_pages, PAGE_SIZE, N_HEADS).transpose(1, 0, 3, 2)

    def step(carry, xs):
        m, l, acc = carry
        pt, cp = xs
        kp = cache_k[layer, pt]
        vp = cache_v[layer, pt]
        s = jnp.einsum('bthd,bshd->bhts', q, kp.astype(q.dtype), preferred_element_type=jnp.float32) * SCALE
        s = s + cq - cp[:, :, None, :]
        m_new = jnp.maximum(m, jnp.max(s, axis=-1))
        corr = jnp.exp(m - m_new)
        p = jnp.exp(s - m_new[..., None])
        l = l * corr + jnp.sum(p, axis=-1)
        acc = acc * corr[..., None] + jnp.einsum('bhts,bshd->bhtd', p.astype(vp.dtype), vp,
                                                 preferred_element_type=jnp.float32)
        return (m_new, l, acc), None

    init = (jnp.full((DB, N_HEADS, T), -jnp.inf, jnp.float32), jnp.zeros((DB, N_HEADS, T), jnp.float32),
            jnp.zeros((DB, N_HEADS, T, HEAD_DIM), jnp.float32))
    (m, l, acc), _ = lax.scan(step, init, (page_table.T, c_pages))
    s = jnp.einsum('bthd,bshd->bhts', q, k_new, preferred_element_type=jnp.float32) * SCALE
    s = s + cq - jnp.swapaxes(c_new, 1, 2)[:, :, None, :]
    s = jnp.where(jnp.tril(jnp.ones((T, T), dtype=bool)), s, -jnp.inf)
    m_new = jnp.maximum(m, jnp.max(s, axis=-1))
    corr = jnp.exp(m - m_new)
    p = jnp.exp(s - m_new[..., None])
    l = l * corr + jnp.sum(p, axis=-1)
    acc = acc * corr[..., None] + jnp.einsum('bhts,bshd->bhtd', p.astype(v_new.dtype), v_new,
                                             preferred_element_type=jnp.float32)
    o = acc / l[..., None]
    return o.transpose(0, 2, 1, 3).reshape(DB, T, ATT_WIDTH).astype(q.dtype)


def branch_merge(x, a_out, b_out, g, w_out):
    ga, gb = jnp.split(g, 2, axis=-1)
    return x + (jax.nn.sigmoid(ga) * a_out + jax.nn.sigmoid(gb) * b_out) @ w_out


def moe(h, w_router, b_router, w_gate_up, b_gate_up, w_down, b_down):
    N = h.shape[0]
    logits = (h @ w_router + b_router).astype(jnp.float32)
    top_v, top_i = lax.top_k(logits, TOP_K)
    gates = jax.nn.softmax(top_v, axis=-1)
    A = N * TOP_K
    e_flat = top_i.reshape(-1)
    tok_flat = jnp.repeat(jnp.arange(N, dtype=jnp.int32), TOP_K)
    g_flat = gates.reshape(-1)
    order = jnp.argsort(e_flat, stable=True)
    e_s, tok_s, g_s = e_flat[order], tok_flat[order], g_flat[order]
    counts = jnp.bincount(e_flat, length=N_EXPERTS)
    starts = jnp.cumsum(counts) - counts
    padded = (counts + MOE_BLOCK - 1) // MOE_BLOCK * MOE_BLOCK
    pend = jnp.cumsum(padded)
    pstarts = pend - padded
    dest = pstarts[e_s] + (jnp.arange(A) - starts[e_s])
    n_blocks = -(-A // MOE_BLOCK) + N_EXPERTS
    P = n_blocks * MOE_BLOCK
    row_tok = jnp.full((P,), N, jnp.int32).at[dest].set(tok_s)
    row_g = jnp.zeros((P,), jnp.float32).at[dest].set(g_s)
    block_e = jnp.minimum(jnp.searchsorted(pend, jnp.arange(n_blocks) * MOE_BLOCK, side='right'),
                          N_EXPERTS - 1)
    h_pad = jnp.concatenate([h, jnp.zeros((1, h.shape[1]), h.dtype)], axis=0)

    def run(args):
        rows, e = args
        xb = h_pad[rows]
        gu = xb @ w_gate_up[e] + b_gate_up[e]
        gate, lin = jnp.split(gu, 2, axis=-1)
        gate = jnp.minimum(gate, SWIGLU_LIMIT)
        lin = jnp.clip(lin, -SWIGLU_LIMIT, SWIGLU_LIMIT)
        act = gate * jax.nn.sigmoid(SWIGLU_ALPHA * gate) * (lin + 1.0)
        return act @ w_down[e] + b_down[e]

    out = lax.map(run, (row_tok.reshape(n_blocks, MOE_BLOCK), block_e)).reshape(P, -1)
    y = jax.ops.segment_sum(out.astype(jnp.float32) * row_g[:, None], row_tok, num_segments=N + 1)[:N]
    return y.astype(h.dtype)


def setup_inputs(seed: int = 0) -> dict:
    key = jax.random.key(seed)
    ks = jax.random.split(key, 32)
    n_pages = PAST_LEN // PAGE_SIZE
    n_used = DEC_BATCH * n_pages
    n_phys = n_used + n_used // 4
    f32 = jnp.float32

    def nrm(k, shape, s):
        return jax.random.normal(k, shape, f32) * s

    page_table = jax.random.permutation(ks[6], n_phys)[:n_used].reshape(DEC_BATCH, n_pages).astype(jnp.int32)
    return {
        'x_prompt': nrm(ks[0], (BATCH, SEQ, D_MODEL), 1.0),
        'x_sample': nrm(ks[1], (DEC_BATCH, DEC_SEQ, D_MODEL), 1.0),
        'cache_k': nrm(ks[2], (DEPTH, n_phys, PAGE_SIZE, N_HEADS, HEAD_DIM), 1.0),
        'cache_v': nrm(ks[3], (DEPTH, n_phys, PAGE_SIZE, N_HEADS, HEAD_DIM), 1.0),
        'cache_logf': jax.nn.log_sigmoid(FORGET_BIAS + nrm(ks[4], (DEPTH, n_phys, PAGE_SIZE, N_HEADS), 1.0)),
        'state_conv': nrm(ks[5], (DEPTH, DEC_BATCH, CONV_WIDTH - 1, CONV_CH), 0.5),
        'page_table': page_table,
        'meta_tokens': nrm(ks[7], (N_META, D_MODEL), 1.0),
        'g_mix': 1.0 + nrm(ks[8], (DEPTH, D_MODEL), 0.02),
        'w_in': nrm(ks[9], (DEPTH, D_MODEL, D_IN), D_MODEL ** -0.5),
        'b_forget': FORGET_BIAS + nrm(ks[10], (DEPTH, N_HEADS), 0.5),
        'w_dw': nrm(ks[11], (DEPTH, CONV_WIDTH, CONV_CH), CONV_WIDTH ** -0.5),
        'b_dw': nrm(ks[12], (DEPTH, CONV_CH), 0.02),
        'ln_g': 1.0 + nrm(ks[13], (DEPTH, CONV_CH), 0.02),
        'ln_b': nrm(ks[14], (DEPTH, CONV_CH), 0.02),
        'w_pw2': nrm(ks[15], (DEPTH, CONV_CH, D_MODEL), CONV_CH ** -0.5),
        'w_att_o': nrm(ks[16], (DEPTH, ATT_WIDTH, D_MODEL), ATT_WIDTH ** -0.5),
        'w_out': nrm(ks[17], (DEPTH, D_MODEL, D_MODEL), D_MODEL ** -0.5),
        'g_ffn': 1.0 + nrm(ks[18], (DEPTH, D_MODEL), 0.02),
        'w_router': nrm(ks[19], (DEPTH, D_MODEL, N_EXPERTS), D_MODEL ** -0.5),
        'b_router': nrm(ks[20], (DEPTH, N_EXPERTS), 0.01),
        'w_gate_up': nrm(ks[21], (DEPTH, N_EXPERTS, D_MODEL, 2 * D_EXPERT), D_MODEL ** -0.5),
        'b_gate_up': nrm(ks[22], (DEPTH, N_EXPERTS, 2 * D_EXPERT), 0.02),
        'w_down': nrm(ks[23], (DEPTH, N_EXPERTS, D_EXPERT, D_MODEL), D_EXPERT ** -0.5),
        'b_down': nrm(ks[24], (DEPTH, N_EXPERTS, D_MODEL), 0.02),
        'g_final': 1.0 + nrm(ks[25], (D_MODEL,), 0.02),
    }


def reference(x_prompt, x_sample, cache_k, cache_v, cache_logf, state_conv, page_table, meta_tokens,
              g_mix, w_in, b_forget, w_dw, b_dw, ln_g, ln_b, w_pw2, w_att_o, w_out, g_ffn,
              w_router, b_router, w_gate_up, b_gate_up, w_down, b_down, g_final):
    B = x_prompt.shape[0]
    meta = jnp.broadcast_to(meta_tokens[None].astype(x_prompt.dtype), (B, N_META, D_MODEL))
    xp = jnp.concatenate([meta, x_prompt], axis=1)
    xs = x_sample
    kp_l, vp_l, lfp_l, cp_l, ks_l, vs_l, lfs_l, cs_l = [], [], [], [], [], [], [], []
    for layer in range(DEPTH):
        hp = rms_norm(xp, g_mix[layer])
        q, k, v, lf, u, g = in_project(hp, w_in[layer], b_forget[layer])
        att = fox_prompt(q, k, v, lf) @ w_att_o[layer]
        u_ctx = jnp.concatenate([jnp.zeros((B, CONV_WIDTH - 1, CONV_CH), u.dtype), u], axis=1)
        conv = conv_branch(u_ctx, w_dw[layer], b_dw[layer], ln_g[layer], ln_b[layer], w_pw2[layer])
        xp = branch_merge(xp, conv, att, g, w_out[layer])
        hf = rms_norm(xp, g_ffn[layer])
        xp = xp + moe(hf.reshape(-1, D_MODEL), w_router[layer], b_router[layer], w_gate_up[layer],
                      b_gate_up[layer], w_down[layer], b_down[layer]).reshape(xp.shape)
        kp_l.append(k)
        vp_l.append(v)
        lfp_l.append(lf)
        cp_l.append(u_ctx[:, -(CONV_WIDTH - 1):])
        hs = rms_norm(xs, g_mix[layer])
        qs, kks, vs, lfs, us, gs = in_project(hs, w_in[layer], b_forget[layer])
        att_s = fox_sample(qs, kks, vs, lfs, cache_k, cache_v, cache_logf, page_table, layer) @ w_att_o[layer]
        us_ctx = jnp.concatenate([state_conv[layer].astype(us.dtype), us], axis=1)
        conv_s = conv_branch(us_ctx, w_dw[layer], b_dw[layer], ln_g[layer], ln_b[layer], w_pw2[layer])
        xs = branch_merge(xs, conv_s, att_s, gs, w_out[layer])
        hfs = rms_norm(xs, g_ffn[layer])
        xs = xs + moe(hfs.reshape(-1, D_MODEL), w_router[layer], b_router[layer], w_gate_up[layer],
                      b_gate_up[layer], w_down[layer], b_down[layer]).reshape(xs.shape)
        ks_l.append(kks)
        vs_l.append(vs)
        lfs_l.append(lfs)
        cs_l.append(us_ctx[:, -(CONV_WIDTH - 1):])
    y_prompt = rms_norm(xp, g_final)[:, N_META:]
    y_sample = rms_norm(xs, g_final)
    k_prompt = jnp.stack(kp_l)
    v_prompt = jnp.stack(vp_l)
    logf_prompt = jnp.stack(lfp_l)
    conv_prompt = jnp.stack(cp_l)
    k_sample = jnp.stack(ks_l)
    v_sample = jnp.stack(vs_l)
    logf_sample = jnp.stack(lfs_l)
    conv_sample = jnp.stack(cs_l)
    return (y_prompt, y_sample, k_prompt, v_prompt, logf_prompt, conv_prompt, k_sample, v_sample, logf_sample, conv_sample)
```

```python
import functools

import jax
import jax.numpy as jnp
from jax import lax
from jax.experimental import pallas as pl
from jax.experimental.pallas import tpu as pltpu

F32 = jnp.float32
BF16 = jnp.bfloat16
I32 = jnp.int32

N_META = 16
N_HEADS = 8
HEAD_DIM = 64
ATT_WIDTH = N_HEADS * HEAD_DIM
CONV_WIDTH = 31
N_EXPERTS = 32
TOP_K = 4
SWIGLU_LIMIT = 7.0
SWIGLU_ALPHA = 1.702
NORM_EPS = 1e-5
SCALE = HEAD_DIM ** -0.5
PAGE_SIZE = 128

LANES = 128
HEAD_ROWS = 16
NEG_BIG = -1e30
VMEM_LIMIT = 56 * 1024 * 1024


def _cparams(n_axes):
    return pltpu.CompilerParams(dimension_semantics=("arbitrary",) * n_axes,
                                vmem_limit_bytes=VMEM_LIMIT)


def _dot(a, b):
    return jnp.dot(a, b, preferred_element_type=F32)


def _dot_nt(a, b):
    return lax.dot_general(a, b, (((1,), (1,)), ((), ())), preferred_element_type=F32)


def _split3(x):
    a = x.astype(BF16)
    r = x - a.astype(F32)
    b = r.astype(BF16)
    c = (r - b.astype(F32)).astype(BF16)
    return a, b, c


def _dot_exact_rhs01(x, m):
    a, b, c = _split3(x)
    return _dot(a, m) + _dot(b, m) + _dot(c, m)


def _dot_exact_lhs01(m, x):
    a, b, c = _split3(x)
    return _dot(m, a) + _dot(m, b) + _dot(m, c)


def _log_sigmoid(x):
    return jnp.minimum(x, 0.0) - jnp.log1p(jnp.exp(-jnp.abs(x)))


def _rms(x, g):
    ms = jnp.mean(x * x, axis=-1, keepdims=True)
    return x * lax.rsqrt(ms + NORM_EPS) * g


def _inproj_body(n_a, d_att, d_conv, d_model,
                 xa_ref, xb_ref, gmix_ref, w_ref, wft_ref, bfr_ref, bfc_ref,
                 q_ref, k_ref, v_ref, kb_ref, vb_ref, lf_ref, lft_ref, u_ref, g_ref):
    i = pl.program_id(0)
    x = jnp.where(i < n_a, xa_ref[...], xb_ref[...])
    hb = _rms(x, gmix_ref[...]).astype(BF16)
    o = 0
    q_ref[...] = (_dot(hb, w_ref[:, o:o + d_att]) * SCALE).astype(BF16)
    o += d_att
    kk = _dot(hb, w_ref[:, o:o + d_att])
    k_ref[...] = kk
    kb_ref[...] = kk.astype(BF16)
    o += d_att
    vv = _dot(hb, w_ref[:, o:o + d_att])
    v_ref[...] = vv
    vb_ref[...] = vv.astype(BF16)
    o += d_att
    ua = _dot(hb, w_ref[:, o:o + d_conv])
    ub = _dot(hb, w_ref[:, o + d_conv:o + 2 * d_conv])
    u_ref[...] = ua * jax.nn.sigmoid(ub)
    o += 2 * d_conv
    for c in range(0, 2 * d_model, 512):
        g_ref[:, c:c + 512] = jax.nn.sigmoid(_dot(hb, w_ref[:, o + c:o + c + 512])).astype(BF16)
    o += 2 * d_model
    lf_ref[...] = _log_sigmoid(_dot(hb, w_ref[:, o:o + LANES]) + bfr_ref[...])
    lft_ref[...] = _log_sigmoid(_dot_nt(wft_ref[...], hb) + bfc_ref[:, 0:1])


def _in_project(xa, xb, n_a, n_b, tm, gmix, w_all, wft, bfr, bfc):
    d_model = xa.shape[1]
    n_t = n_a + n_b
    n = n_t * tm
    d_att, d_conv = ATT_WIDTH, d_model // 2
    wcols = w_all.shape[1]
    row = lambda i: (i, 0)
    full = lambda i: (0, 0)
    out_shape = (
        jax.ShapeDtypeStruct((n, d_att), BF16),
        jax.ShapeDtypeStruct((n, d_att), F32),
        jax.ShapeDtypeStruct((n, d_att), F32),
        jax.ShapeDtypeStruct((n, d_att), BF16),
        jax.ShapeDtypeStruct((n, d_att), BF16),
        jax.ShapeDtypeStruct((n, LANES), F32),
        jax.ShapeDtypeStruct((HEAD_ROWS, n), F32),
        jax.ShapeDtypeStruct((n, d_conv), F32),
        jax.ShapeDtypeStruct((n, 2 * d_model), BF16),
    )
    out_specs = (
        pl.BlockSpec((tm, d_att), row), pl.BlockSpec((tm, d_att), row), pl.BlockSpec((tm, d_att), row),
        pl.BlockSpec((tm, d_att), row), pl.BlockSpec((tm, d_att), row),
        pl.BlockSpec((tm, LANES), row), pl.BlockSpec((HEAD_ROWS, tm), lambda i: (0, i)),
        pl.BlockSpec((tm, d_conv), row), pl.BlockSpec((tm, 2 * d_model), row),
    )
    in_specs = [
        pl.BlockSpec((tm, d_model), lambda i: (jnp.minimum(i, n_a - 1), 0)),
        pl.BlockSpec((tm, d_model), lambda i: (jnp.maximum(i - n_a, 0), 0)),
        pl.BlockSpec((1, d_model), full),
        pl.BlockSpec((d_model, wcols), full),
        pl.BlockSpec((HEAD_ROWS, d_model), full),
        pl.BlockSpec((1, LANES), full),
        pl.BlockSpec((HEAD_ROWS, LANES), full),
    ]
    return pl.pallas_call(
        functools.partial(_inproj_body, n_a, d_att, d_conv, d_model),
        grid=(n_t,), in_specs=in_specs, out_specs=out_specs, out_shape=out_shape,
        compiler_params=_cparams(1), name="in_project",
    )(xa, xb, gmix, w_all, wft, bfr, bfc)


def _cumsum_body(n_a, seg, lft_ref, tri_ref, init_ref, ct_ref, carry_ref):
    i = pl.program_id(0)
    is_a = i < n_a
    start = jnp.logical_and(is_a, i % seg == 0)
    carry = jnp.where(start, init_ref[...], carry_ref[...])
    carry = jnp.where(is_a, carry, 0.0)
    c = _dot_exact_rhs01(lft_ref[...], tri_ref[0]) + carry[:, 0:1]
    ct_ref[...] = c
    tm = c.shape[1]
    carry_ref[...] = jnp.broadcast_to(c[:, tm - 1:tm], carry_ref.shape)


def _cumsum_tokens(lft, tris, init, n_a, seg, tm):
    n = lft.shape[1]
    return pl.pallas_call(
        functools.partial(_cumsum_body, n_a, seg),
        grid=(n // tm,),
        in_specs=[pl.BlockSpec((HEAD_ROWS, tm), lambda i: (0, i)),
                  pl.BlockSpec((1, tm, tm), lambda i: (jnp.where(i < n_a, 0, 1), 0, 0)),
                  pl.BlockSpec((HEAD_ROWS, LANES), lambda i: (0, 0))],
        out_specs=pl.BlockSpec((HEAD_ROWS, tm), lambda i: (0, i)),
        out_shape=jax.ShapeDtypeStruct((HEAD_ROWS, n), F32),
        scratch_shapes=[pltpu.VMEM((HEAD_ROWS, LANES), F32)],
        compiler_params=_cparams(1), name="decay_cumsum",
    )(lft, tris, init)


def _attn_body(tq, q_ref, k_ref, v_ref, ct_ref, km_ref, vm_ref, ctm_ref, o_ref):
    hp = pl.program_id(1)
    i = pl.program_id(2)
    lane = lax.broadcasted_iota(I32, (1, LANES), 1)
    q2 = q_ref[...]
    km = km_ref[...]
    vm = vm_ref[...]
    rows = lax.broadcasted_iota(I32, (tq, tq), 0)
    cols = lax.broadcasted_iota(I32, (tq, tq), 1)
    outs = []
    for hh in range(2):
        in_head = (lane < HEAD_DIM) if hh == 0 else (lane >= HEAD_DIM)
        qh = jnp.where(in_head, q2, jnp.zeros_like(q2))
        head = 2 * hp + hh
        s = _dot_nt(qh, km) - ctm_ref[pl.ds(head, 1), :]
        m = jnp.max(s, axis=1, keepdims=True)
        p = jnp.exp(s - m)
        l = jnp.sum(p, axis=1, keepdims=True)
        acc = _dot(p.astype(BF16), vm)

        def tile(j, carry, diag):
            m, l, acc = carry
            k0 = pl.multiple_of(j * tq, tq)
            s = _dot_nt(qh, k_ref[pl.ds(k0, tq), :]) - ct_ref[pl.ds(head, 1), pl.ds(k0, tq)]
            if diag:
                s = jnp.where(cols <= rows, s, NEG_BIG)
            mn = jnp.maximum(m, jnp.max(s, axis=1, keepdims=True))
            a = jnp.exp(m - mn)
            p = jnp.exp(s - mn)
            l = a * l + jnp.sum(p, axis=1, keepdims=True)
            acc = a * acc + _dot(p.astype(BF16), v_ref[pl.ds(k0, tq), :])
            return mn, l, acc

        carry = lax.fori_loop(0, i, lambda j, c: tile(j, c, False), (m, l, acc))
        m, l, acc = tile(i, carry, True)
        outs.append(acc / l)
    o_ref[...] = jnp.where(lane < HEAD_DIM, outs[0], outs[1]).astype(o_ref.dtype)


def _prompt_attention(q, kb, vb, ct, kb_m, vb_m, ct_m, n_batch, seq, tq):
    n_hp = ATT_WIDTH // LANES
    n_q = seq // tq
    return pl.pallas_call(
        functools.partial(_attn_body, tq),
        grid=(n_batch, n_hp, n_q),
        in_specs=[pl.BlockSpec((tq, LANES), lambda b, h, i: (b * n_q + i, h)),
                  pl.BlockSpec((seq, LANES), lambda b, h, i: (b, h)),
                  pl.BlockSpec((seq, LANES), lambda b, h, i: (b, h)),
                  pl.BlockSpec((HEAD_ROWS, seq), lambda b, h, i: (0, b)),
                  pl.BlockSpec((N_META, LANES), lambda b, h, i: (0, h)),
                  pl.BlockSpec((N_META, LANES), lambda b, h, i: (0, h)),
                  pl.BlockSpec((HEAD_ROWS, N_META), lambda b, h, i: (0, 0))],
        out_specs=pl.BlockSpec((tq, LANES), lambda b, h, i: (b * n_q + i, h)),
        out_shape=jax.ShapeDtypeStruct((n_batch * seq, ATT_WIDTH), BF16),
        compiler_params=_cparams(3), name="prompt_attention",
    )(q, kb, vb, ct, kb_m, vb_m, ct_m)


def _past_cumsum_body(n_pages, pt_ref, *refs):
    page_refs = refs[:n_pages]
    out_ref = refs[n_pages]
    rows = n_pages * N_HEADS
    x = jnp.concatenate([r[0] for r in page_refs], axis=0)
    li = lax.broadcasted_iota(I32, (LANES, LANES), 0)
    lj = lax.broadcasted_iota(I32, (LANES, LANES), 1)
    upper = (li <= lj).astype(BF16)
    within = _dot_exact_rhs01(x, upper)
    ri = lax.broadcasted_iota(I32, (rows, rows), 0)
    rj = lax.broadcasted_iota(I32, (rows, rows), 1)
    earlier = jnp.logical_and(ri % N_HEADS == rj % N_HEADS, rj < ri).astype(BF16)
    totals = jnp.broadcast_to(within[:, LANES - 1:LANES], (rows, LANES))
    before = _dot_exact_lhs01(earlier, totals)
    out_ref[0] = (within + before).reshape(n_pages, N_HEADS, LANES)


def _past_cumsum(page_table_flat, lf_pages_t, n_seq, n_pages):
    def pmap(p):
        return lambda b, pt: (pt[b * n_pages + p], 0, 0)
    grid_spec = pltpu.PrefetchScalarGridSpec(
        num_scalar_prefetch=1, grid=(n_seq,),
        in_specs=[pl.BlockSpec((1, N_HEADS, PAGE_SIZE), pmap(p)) for p in range(n_pages)],
        out_specs=pl.BlockSpec((1, n_pages, N_HEADS, PAGE_SIZE), lambda b, pt: (b, 0, 0, 0)))
    return pl.pallas_call(
        functools.partial(_past_cumsum_body, n_pages),
        grid_spec=grid_spec,
        out_shape=jax.ShapeDtypeStruct((n_seq, n_pages, N_HEADS, PAGE_SIZE), F32),
        compiler_params=_cparams(1), name="past_decay_cumsum",
    )(page_table_flat, *([lf_pages_t] * n_pages))


def _sample_attn_body(pps, n_new, pt_ref, q_ref, kn_ref, vn_ref, cnr_ref, cnc_ref, ctot_ref, cp_ref,
                      ck_ref, cv_ref, o_ref, m_ref, l_ref, acc_ref, kbuf, vbuf, sem):
    g = pl.program_id(1)
    n_g = pl.num_programs(1)
    step = pl.program_id(0) * n_g + g
    n_steps = pl.num_programs(0) * n_g
    slot = step % 2

    def page_copies(s, sl):
        out = []
        for p in range(pps):
            page = pt_ref[s * pps + p]
            out.append(pltpu.make_async_copy(ck_ref.at[page], kbuf.at[sl, p], sem.at[sl, 0]))
            out.append(pltpu.make_async_copy(cv_ref.at[page], vbuf.at[sl, p], sem.at[sl, 1]))
        return out

    @pl.when(step == 0)
    def _():
        for c in page_copies(step, slot):
            c.start()

    @pl.when(step + 1 < n_steps)
    def _():
        for c in page_copies(step + 1, 1 - slot):
            c.start()

    for c in page_copies(step, slot):
        c.wait()

    n_rows = N_HEADS * n_new
    row = lax.broadcasted_iota(I32, (n_rows, ATT_WIDTH), 0)
    col = lax.broadcasted_iota(I32, (n_rows, ATT_WIDTH), 1)
    own_head = (row // n_new) == (col // HEAD_DIM)

    q = q_ref[0].astype(F32)
    qbd = jnp.where(own_head, jnp.concatenate([q] * N_HEADS, axis=0), 0.0).astype(BF16)

    @pl.when(g == 0)
    def _():
        m_ref[...] = jnp.full(m_ref.shape, NEG_BIG, F32)
        l_ref[...] = jnp.zeros(l_ref.shape, F32)
        acc_ref[...] = jnp.zeros(acc_ref.shape, F32)

    kb = kbuf[slot].astype(BF16).reshape(pps * PAGE_SIZE, ATT_WIDTH)
    vb = vbuf[slot].astype(BF16).reshape(pps * PAGE_SIZE, ATT_WIDTH)
    cp = cp_ref[0]
    bias = jnp.concatenate(
        [jnp.broadcast_to(cp[p][:, None, :], (N_HEADS, n_new, PAGE_SIZE)).reshape(n_rows, PAGE_SIZE)
         for p in range(pps)], axis=1)
    s = _dot_nt(qbd, kb) - bias
    m_old = m_ref[:, 0:1]
    mn = jnp.maximum(m_old, jnp.max(s, axis=1, keepdims=True))
    a = jnp.exp(m_old - mn)
    p = jnp.exp(s - mn)
    l_new = a * l_ref[:, 0:1] + jnp.sum(p, axis=1, keepdims=True)
    acc_new = a * acc_ref[...] + _dot(p.astype(BF16), vb)
    m_ref[...] = jnp.broadcast_to(mn, m_ref.shape)
    l_ref[...] = jnp.broadcast_to(l_new, l_ref.shape)
    acc_ref[...] = acc_new

    @pl.when(g == n_g - 1)
    def _():
        cn_col = cnc_ref[0][:, 0:1]
        m_past = mn + ctot_ref[0][:, 0:1] + cn_col
        zpad = jnp.zeros((HEAD_ROWS - n_new, ATT_WIDTH), F32)
        kn = jnp.concatenate([kn_ref[0], zpad], axis=0).astype(BF16)
        vn = jnp.concatenate([vn_ref[0], zpad], axis=0).astype(BF16)
        sn = _dot_nt(qbd, kn) + cn_col - cnr_ref[0]
        r2 = lax.broadcasted_iota(I32, sn.shape, 0) % n_new
        c2 = lax.broadcasted_iota(I32, sn.shape, 1)
        sn = jnp.where(c2 <= r2, sn, NEG_BIG)
        m2 = jnp.maximum(m_past, jnp.max(sn, axis=1, keepdims=True))
        a2 = jnp.exp(m_past - m2)
        pn = jnp.exp(sn - m2)
        l2 = a2 * l_new + jnp.sum(pn, axis=1, keepdims=True)
        acc2 = a2 * acc_new + _dot(pn.astype(BF16), vn)
        o = jnp.where(own_head, acc2 / l2, 0.0)
        o_ref[0] = jnp.sum(o.reshape(N_HEADS, n_new, ATT_WIDTH), axis=0)


def _sample_attention(page_table_flat, cache_k, cache_v, q_s, k_s, v_s, cn_rows, cn_col, ctot_col, cpast,
                      n_seq, n_pages, n_new, pps):
    n_g = n_pages // pps
    n_rows = N_HEADS * n_new

    seq3 = lambda b, g, pt: (b, 0, 0)
    grid_spec = pltpu.PrefetchScalarGridSpec(
        num_scalar_prefetch=1, grid=(n_seq, n_g),
        in_specs=[pl.BlockSpec((1, n_new, ATT_WIDTH), seq3),
                  pl.BlockSpec((1, n_new, ATT_WIDTH), seq3),
                  pl.BlockSpec((1, n_new, ATT_WIDTH), seq3),
                  pl.BlockSpec((1, n_rows, HEAD_ROWS), seq3),
                  pl.BlockSpec((1, n_rows, 1), seq3),
                  pl.BlockSpec((1, n_rows, 1), seq3),
                  pl.BlockSpec((1, pps, N_HEADS, PAGE_SIZE), lambda b, g, pt: (b, g, 0, 0)),
                  pl.BlockSpec(memory_space=pl.ANY), pl.BlockSpec(memory_space=pl.ANY)],
        out_specs=pl.BlockSpec((1, n_new, ATT_WIDTH), seq3),
        scratch_shapes=[pltpu.VMEM((n_rows, LANES), F32), pltpu.VMEM((n_rows, LANES), F32),
                        pltpu.VMEM((n_rows, ATT_WIDTH), F32),
                        pltpu.VMEM((2, pps, PAGE_SIZE, ATT_WIDTH), F32),
                        pltpu.VMEM((2, pps, PAGE_SIZE, ATT_WIDTH), F32),
                        pltpu.SemaphoreType.DMA((2, 2))])
    return pl.pallas_call(
        functools.partial(_sample_attn_body, pps, n_new),
        grid_spec=grid_spec,
        out_shape=jax.ShapeDtypeStruct((n_seq, n_new, ATT_WIDTH), F32),
        compiler_params=_cparams(2), name="sample_attention",
    )(page_table_flat, q_s, k_s, v_s, cn_rows, cn_col, ctot_col, cpast, cache_k, cache_v)


HALO = 32


def _conv_tail(y, bdw, lng, lnb):
    y = y + bdw
    mu = jnp.mean(y, axis=-1, keepdims=True)
    d = y - mu
    var = jnp.mean(d * d, axis=-1, keepdims=True)
    yn = d * lax.rsqrt(var + NORM_EPS) * lng + lnb
    return yn * jax.nn.sigmoid(yn)


def _conv_prompt_body(tm, u_ref, halo_ref, st_ref, wdw_ref, bdw_ref, lng_ref, lnb_ref, cs_ref, ctx_ref):
    i = pl.program_id(1)
    ctx_ref[0:HALO, :] = jnp.where(i == 0, st_ref[...], halo_ref[...])
    ctx_ref[HALO:HALO + tm, :] = u_ref[...]
    off = HALO - (CONV_WIDTH - 1)
    y = jnp.zeros((tm, u_ref.shape[1]), F32)
    for k in range(CONV_WIDTH):
        y = y + ctx_ref[off + k:off + k + tm, :] * wdw_ref[k:k + 1, :]
    cs_ref[...] = _conv_tail(y, bdw_ref[...], lng_ref[...], lnb_ref[...]).astype(cs_ref.dtype)


def _conv_prompt(u, state0, wdw, bdw, lng, lnb, n_batch, seq, tm):
    ch = u.shape[1]
    n_i = seq // tm
    hb = tm // HALO
    full = lambda b, i: (0, 0)
    return pl.pallas_call(
        functools.partial(_conv_prompt_body, tm),
        grid=(n_batch, n_i),
        in_specs=[pl.BlockSpec((tm, ch), lambda b, i: (b * n_i + i, 0)),
                  pl.BlockSpec((HALO, ch), lambda b, i: (jnp.maximum((b * n_i + i) * hb - 1, 0), 0)),
                  pl.BlockSpec((HALO, ch), full),
                  pl.BlockSpec((CONV_WIDTH + 1, ch), full),
                  pl.BlockSpec((1, ch), full), pl.BlockSpec((1, ch), full), pl.BlockSpec((1, ch), full)],
        out_specs=pl.BlockSpec((tm, ch), lambda b, i: (b * n_i + i, 0)),
        out_shape=jax.ShapeDtypeStruct((n_batch * seq, ch), BF16),
        scratch_shapes=[pltpu.VMEM((HALO + tm, ch), F32)],
        compiler_params=_cparams(2), name="conv_prompt",
    )(u, u, state0, wdw, bdw, lng, lnb)


def _conv_sample_body(n_new, st_ref, u_ref, wdw_ref, bdw_ref, lng_ref, lnb_ref, cs_ref, st_out_ref, ctx_ref):
    n_st = CONV_WIDTH - 1
    off = HALO - n_st
    ctx_ref[:, off:HALO, :] = st_ref[...]
    ctx_ref[:, HALO:HALO + n_new, :] = u_ref[...]
    sb, _, ch = u_ref.shape
    y = jnp.zeros((sb, n_new, ch), F32)
    for k in range(CONV_WIDTH):
        y = y + ctx_ref[:, off + k:off + k + n_new, :] * wdw_ref[k:k + 1, :]
    cs_ref[...] = _conv_tail(y, bdw_ref[...], lng_ref[...], lnb_ref[...]).astype(cs_ref.dtype)
    st_out_ref[...] = ctx_ref[:, off + n_new:off + n_new + n_st, :]


def _conv_sample(state, u, wdw, bdw, lng, lnb, sb):
    n_seq, n_new, ch = u.shape
    n_st = CONV_WIDTH - 1
    full = lambda i: (0, 0)
    blk = lambda i: (i, 0, 0)
    return pl.pallas_call(
        functools.partial(_conv_sample_body, n_new),
        grid=(n_seq // sb,),
        in_specs=[pl.BlockSpec((sb, n_st, ch), blk), pl.BlockSpec((sb, n_new, ch), blk),
                  pl.BlockSpec((CONV_WIDTH + 1, ch), full),
                  pl.BlockSpec((1, ch), full), pl.BlockSpec((1, ch), full), pl.BlockSpec((1, ch), full)],
        out_specs=(pl.BlockSpec((sb, n_new, ch), blk), pl.BlockSpec((sb, n_st, ch), blk)),
        out_shape=(jax.ShapeDtypeStruct((n_seq, n_new, ch), BF16),
                   jax.ShapeDtypeStruct((n_seq, n_st, ch), F32)),
        scratch_shapes=[pltpu.VMEM((sb, HALO + n_new, ch), F32)],
        compiler_params=_cparams(1), name="conv_sample",
    )(state, u, wdw, bdw, lng, lnb)


def _merge_body(n_a, d_model, xa_ref, xb_ref, csa_ref, csb_ref, oa_ref, ob_ref, g_ref,
                wpw_ref, wao_ref, wout_ref, gffn_ref, wr12_ref, wr1_ref, br_ref,
                xm_ref, hf_ref, lg_ref):
    i = pl.program_id(0)
    is_a = i < n_a
    x = jnp.where(is_a, xa_ref[...], xb_ref[...])
    cs = jnp.where(is_a, csa_ref[...], csb_ref[...])
    o = jnp.where(is_a, oa_ref[...], ob_ref[...])
    conv = _dot(cs, wpw_ref[...])
    att = _dot(o, wao_ref[...])
    mix = g_ref[:, 0:d_model].astype(F32) * conv + g_ref[:, d_model:2 * d_model].astype(F32) * att
    xm = x + _dot(mix.astype(BF16), wout_ref[...])
    xm_ref[...] = xm
    hf = _rms(xm, gffn_ref[...])
    hf_ref[...] = hf
    h1 = hf.astype(BF16)
    h2 = (hf - h1.astype(F32)).astype(BF16)
    big = _dot(h1, wr12_ref[...])
    lg_ref[...] = big[:, 0:LANES] + big[:, LANES:2 * LANES] + _dot(h2, wr1_ref[...]) + br_ref[...]


def _merge(xa, xb, csa, csb, oa, ob, g, wpw, wao, wout, gffn, wr12, wr1, br, n_a, tm):
    d_model = xa.shape[1]
    n = g.shape[0]
    n_t = n // tm
    row = lambda i: (i, 0)
    ra = lambda i: (jnp.minimum(i, n_a - 1), 0)
    rb = lambda i: (jnp.maximum(i - n_a, 0), 0)
    full = lambda i: (0, 0)
    dc, da = csa.shape[1], oa.shape[1]
    return pl.pallas_call(
        functools.partial(_merge_body, n_a, d_model),
        grid=(n_t,),
        in_specs=[pl.BlockSpec((tm, d_model), ra), pl.BlockSpec((tm, d_model), rb),
                  pl.BlockSpec((tm, dc), ra), pl.BlockSpec((tm, dc), rb),
                  pl.BlockSpec((tm, da), ra), pl.BlockSpec((tm, da), rb),
                  pl.BlockSpec((tm, 2 * d_model), row),
                  pl.BlockSpec((dc, d_model), full), pl.BlockSpec((da, d_model), full),
                  pl.BlockSpec((d_model, d_model), full), pl.BlockSpec((1, d_model), full),
                  pl.BlockSpec((d_model, 2 * LANES), full), pl.BlockSpec((d_model, LANES), full),
                  pl.BlockSpec((1, LANES), full)],
        out_specs=(pl.BlockSpec((tm, d_model), row), pl.BlockSpec((tm, d_model), row),
                   pl.BlockSpec((tm, LANES), row)),
        out_shape=(jax.ShapeDtypeStruct((n, d_model), F32), jax.ShapeDtypeStruct((n, d_model), F32),
                   jax.ShapeDtypeStruct((n, LANES), F32)),
        compiler_params=_cparams(1), name="merge_router",
    )(xa, xb, csa, csb, oa, ob, g, wpw, wao, wout, gffn, wr12, wr1, br)


ROUTE_IDX, ROUTE_RANK, ROUTE_GATE = 0, TOP_K, 2 * TOP_K


def _route_body(lg_ref, route_ref, cnt_ref, carry_ref):
    i = pl.program_id(0)

    @pl.when(i == 0)
    def _():
        carry_ref[...] = jnp.zeros(carry_ref.shape, F32)

    lg = lg_ref[...]
    tm = lg.shape[0]
    lane = lax.broadcasted_iota(I32, lg.shape, 1).astype(F32)
    vals, sels, idxs = [], [], []
    for _ in range(TOP_K):
        mk = jnp.max(lg, axis=1, keepdims=True)
        ik = jnp.min(jnp.where(lg == mk, lane, float(LANES)), axis=1, keepdims=True)
        sel = lane == ik
        vals.append(mk)
        idxs.append(ik)
        sels.append(sel)
        lg = jnp.where(sel, -3e38, lg)
    onehot = sum(s.astype(F32) for s in sels)
    es = [jnp.exp(v - vals[0]) for v in vals]
    den = sum(es)
    ri = lax.broadcasted_iota(I32, (tm, tm), 0)
    rj = lax.broadcasted_iota(I32, (tm, tm), 1)
    before = (rj < ri).astype(BF16)
    carry = carry_ref[0:1, :]
    cum = _dot(before, onehot.astype(BF16)) + carry
    route = jnp.zeros(lg.shape, F32)
    for k in range(TOP_K):
        rank = jnp.sum(jnp.where(sels[k], cum, 0.0), axis=1, keepdims=True)
        route = jnp.where(lane == ROUTE_IDX + k, idxs[k], route)
        route = jnp.where(lane == ROUTE_RANK + k, rank, route)
        route = jnp.where(lane == ROUTE_GATE + k, es[k] / den, route)
    route_ref[...] = route
    total = carry + jnp.sum(onehot, axis=0, keepdims=True)
    carry_ref[...] = jnp.broadcast_to(total, carry_ref.shape)
    cnt_ref[...] = jnp.broadcast_to(total, cnt_ref.shape)


def _route(logits, tm):
    n = logits.shape[0]
    return pl.pallas_call(
        _route_body,
        grid=(n // tm,),
        in_specs=[pl.BlockSpec((tm, LANES), lambda i: (i, 0))],
        out_specs=(pl.BlockSpec((tm, LANES), lambda i: (i, 0)), pl.BlockSpec((8, LANES), lambda i: (0, 0))),
        out_shape=(jax.ShapeDtypeStruct((n, LANES), F32), jax.ShapeDtypeStruct((8, LANES), F32)),
        scratch_shapes=[pltpu.VMEM((8, LANES), F32)],
        compiler_params=_cparams(1), name="route_topk",
    )(logits)


def _dispatch_body(tm, te, n_blocks, pend_ref, dest_ref, hf_ref, xs_ref, zero_ref, sem):
    i = pl.program_id(0)

    def row_copy(tok, d):
        return pltpu.make_async_copy(hf_ref.at[pl.ds(tok, 1), :], xs_ref.at[pl.ds(d, 1), :], sem)

    @pl.when(i == 0)
    def _():
        zero_ref[...] = jnp.zeros(zero_ref.shape, F32)

        def tail_copy(e):
            start = pl.multiple_of(pend_ref[e + 1] - te, te)
            return pltpu.make_async_copy(zero_ref, xs_ref.at[pl.ds(start, te), :], sem)

        def fill(e, c):
            @pl.when(pend_ref[e + 1] > pend_ref[e])
            def _():
                tail_copy(e).start()
            return c

        def drain(e, c):
            @pl.when(pend_ref[e + 1] > pend_ref[e])
            def _():
                tail_copy(e).wait()
            return c

        lax.fori_loop(0, N_EXPERTS, fill, 0)
        lax.fori_loop(0, N_EXPERTS, drain, 0)

        def unused_copy(bk):
            return pltpu.make_async_copy(zero_ref, xs_ref.at[pl.ds(pl.multiple_of(bk * te, te), te), :], sem)

        def fill_unused(bk, c):
            unused_copy(bk).start()
            return c

        def drain_unused(bk, c):
            unused_copy(bk).wait()
            return c

        n_used = lax.div(pend_ref[N_EXPERTS], te)
        lax.fori_loop(n_used, n_blocks, fill_unused, 0)
        lax.fori_loop(n_used, n_blocks, drain_unused, 0)

    base = i * tm

    def issue(r, c):
        for k in range(TOP_K):
            row_copy(base + r, dest_ref[r * TOP_K + k]).start()
        return c

    def drain_rows(r, c):
        for k in range(TOP_K):
            row_copy(base + r, dest_ref[r * TOP_K + k]).wait()
        return c

    lax.fori_loop(0, tm, issue, 0)
    lax.fori_loop(0, tm, drain_rows, 0)


def _dispatch(pend, dest_flat, hf, n_rows, tm, te):
    n, d_model = hf.shape
    grid_spec = pltpu.PrefetchScalarGridSpec(
        num_scalar_prefetch=1, grid=(n // tm,),
        in_specs=[pl.BlockSpec((tm * TOP_K,), lambda i, pe: (i,), memory_space=pltpu.SMEM),
                  pl.BlockSpec(memory_space=pl.ANY)],
        out_specs=pl.BlockSpec(memory_space=pl.ANY),
        scratch_shapes=[pltpu.VMEM((te, d_model), F32), pltpu.SemaphoreType.DMA(())])
    return pl.pallas_call(
        functools.partial(_dispatch_body, tm, te, n_rows // te),
        grid_spec=grid_spec,
        out_shape=jax.ShapeDtypeStruct((n_rows, d_model), F32),
        compiler_params=_cparams(1), name="moe_dispatch",
    )(pend, dest_flat, hf)


def _expert_body(d_exp, be_ref, nu_ref, x_ref, wgu_ref, bgu_ref, wdn_ref, bdn_ref, y_ref, wgu_b, wdn_b):
    i = pl.program_id(0)
    used = i < nu_ref[0]
    prev = be_ref[jnp.maximum(i - 1, 0)]
    fresh = jnp.logical_and(used, jnp.logical_or(i == 0, be_ref[i] != prev))

    @pl.when(fresh)
    def _():
        wgu_b[...] = wgu_ref[0].astype(BF16)
        wdn_b[...] = wdn_ref[0].astype(BF16)

    @pl.when(used)
    def _():
        xb = x_ref[...].astype(BF16)
        gu = _dot(xb, wgu_b[...]) + bgu_ref[0]
        gate = jnp.minimum(gu[:, 0:d_exp], SWIGLU_LIMIT)
        lin = jnp.clip(gu[:, d_exp:2 * d_exp], -SWIGLU_LIMIT, SWIGLU_LIMIT)
        act = gate * jax.nn.sigmoid(SWIGLU_ALPHA * gate) * (lin + 1.0)
        y_ref[...] = _dot(act.astype(BF16), wdn_b[...]) + bdn_ref[0]

    @pl.when(jnp.logical_not(used))
    def _():
        y_ref[...] = jnp.zeros(y_ref.shape, F32)


def _experts(block_expert, n_used, xs, wgu, bgu, wdn, bdn, te):
    n_rows, d_model = xs.shape
    d_exp = wdn.shape[1]
    n_blocks = n_rows // te
    rowmap = lambda i, be, nu: (jnp.minimum(i, nu[0] - 1), 0)
    emap = lambda i, be, nu: (be[i], 0, 0)
    grid_spec = pltpu.PrefetchScalarGridSpec(
        num_scalar_prefetch=2, grid=(n_blocks,),
        in_specs=[pl.BlockSpec((te, d_model), rowmap),
                  pl.BlockSpec((1, d_model, 2 * d_exp), emap),
                  pl.BlockSpec((1, 1, 2 * d_exp), emap),
                  pl.BlockSpec((1, d_exp, d_model), emap),
                  pl.BlockSpec((1, 1, d_model), emap)],
        out_specs=pl.BlockSpec((te, d_model), lambda i, be, nu: (i, 0)),
        scratch_shapes=[pltpu.VMEM((d_model, 2 * d_exp), BF16), pltpu.VMEM((d_exp, d_model), BF16)])
    return pl.pallas_call(
        functools.partial(_expert_body, d_exp),
        grid_spec=grid_spec,
        out_shape=jax.ShapeDtypeStruct((n_rows, d_model), F32),
        compiler_params=_cparams(1), name="moe_experts",
    )(block_expert, n_used, xs, wgu, bgu, wdn, bdn)


def _combine_body(n_a, tm, dest_ref, ys_ref, route_ref, xm_ref, gfin_ref, ya_ref, yb_ref, buf_ref, sem):
    i = pl.program_id(0)

    def row_copy(r, k):
        d = dest_ref[r * TOP_K + k]
        return pltpu.make_async_copy(ys_ref.at[pl.ds(d, 1), :], buf_ref.at[k, pl.ds(r, 1), :], sem)

    def issue(r, c):
        for k in range(TOP_K):
            row_copy(r, k).start()
        return c

    def drain(r, c):
        for k in range(TOP_K):
            row_copy(r, k).wait()
        return c

    lax.fori_loop(0, tm, issue, 0)
    lax.fori_loop(0, tm, drain, 0)
    route = route_ref[...]
    y = jnp.zeros(xm_ref.shape, F32)
    for k in range(TOP_K):
        y = y + buf_ref[k] * route[:, ROUTE_GATE + k:ROUTE_GATE + k + 1]
    out = _rms(xm_ref[...] + y, gfin_ref[...])

    @pl.when(i < n_a)
    def _():
        ya_ref[...] = out

    @pl.when(i >= n_a)
    def _():
        yb_ref[...] = out


def _combine(dest_flat, ys, route, xm, gfin, n_a, tm):
    n, d_model = xm.shape
    n_t = n // tm
    n_b = n_t - n_a
    return pl.pallas_call(
        functools.partial(_combine_body, n_a, tm),
        grid=(n_t,),
        in_specs=[pl.BlockSpec((tm * TOP_K,), lambda i: (i,), memory_space=pltpu.SMEM),
                  pl.BlockSpec(memory_space=pl.ANY),
                  pl.BlockSpec((tm, LANES), lambda i: (i, 0)),
                  pl.BlockSpec((tm, d_model), lambda i: (i, 0)),
                  pl.BlockSpec((1, d_model), lambda i: (0, 0))],
        out_specs=(pl.BlockSpec((tm, d_model), lambda i: (jnp.minimum(i, n_a - 1), 0)),
                   pl.BlockSpec((tm, d_model), lambda i: (jnp.maximum(i - n_a, 0), 0))),
        out_shape=(jax.ShapeDtypeStruct((n_a * tm, d_model), F32),
                   jax.ShapeDtypeStruct((n_b * tm, d_model), F32)),
        scratch_shapes=[pltpu.VMEM((TOP_K, tm, d_model), F32), pltpu.SemaphoreType.DMA(())],
        compiler_params=_cparams(1), name="moe_combine",
    )(dest_flat, ys, route, xm, gfin)


def _prep_in_weights(w_in, b_forget, d_model):
    d_conv = d_model // 2
    s0, s1, s2 = ATT_WIDTH, 2 * ATT_WIDTH, 3 * ATT_WIDTH
    s3 = s2 + N_HEADS
    s4 = s3 + 2 * d_conv
    wf = w_in[:, s2:s3]
    wf_pad = jnp.pad(wf, ((0, 0), (0, LANES - N_HEADS)))
    w_all = jnp.concatenate([w_in[:, :s2], w_in[:, s3:s4], w_in[:, s4:], wf_pad], axis=1).astype(BF16)
    wft = jnp.pad(wf.T, ((0, HEAD_ROWS - N_HEADS), (0, 0))).astype(BF16)
    bfr = jnp.pad(b_forget, (0, LANES - N_HEADS))[None, :]
    bfc = jnp.broadcast_to(jnp.pad(b_forget, (0, HEAD_ROWS - N_HEADS))[:, None], (HEAD_ROWS, LANES))
    return w_all, wft, bfr, bfc


def _tri_matrices(tm, group):
    i = jnp.arange(tm)
    upper = i[:, None] <= i[None, :]
    same = (i[:, None] // group) == (i[None, :] // group)
    return jnp.stack([upper, jnp.logical_and(upper, same)]).astype(BF16)


def _forward(x_prompt, x_sample, cache_k, cache_v, cache_logf, state_conv, page_table, meta_tokens,
             g_mix, w_in, b_forget, w_dw, b_dw, ln_g, ln_b, w_pw2, w_att_o, w_out, g_ffn,
             w_router, b_router, w_gate_up, b_gate_up, w_down, b_down, g_final,
             tm, tq, pps, sb, t_route, t_disp, te, t_comb):
    n_batch, seq, d_model = x_prompt.shape
    n_seq, n_new, _ = x_sample.shape
    n_pages = page_table.shape[1]
    n_phys = cache_k.shape[1]
    d_conv = d_model // 2
    n_p = n_batch * seq
    n_s = n_seq * n_new
    n_a = n_p // tm
    layer = 0

    xa = x_prompt.reshape(n_p, d_model)
    xb = x_sample.reshape(n_s, d_model)
    gmix = g_mix[layer][None, :]
    w_all, wft, bfr, bfc = _prep_in_weights(w_in[layer], b_forget[layer], d_model)

    meta = meta_tokens.astype(F32)
    (_, k_m, v_m, kb_m, vb_m, lf_m, lft_m, u_m, _) = _in_project(
        meta, meta, 1, 0, N_META, gmix, w_all, wft, bfr, bfc)
    (q, k, v, kb, vb, lf, lft, u, g) = _in_project(xa, xb, n_a, n_s // tm, tm, gmix, w_all, wft, bfr, bfc)

    tri_m = _tri_matrices(N_META, N_META)
    ct_m = _cumsum_tokens(lft_m, tri_m, jnp.zeros((HEAD_ROWS, LANES), F32), 1, 1, N_META)
    init = jnp.broadcast_to(ct_m[:, N_META - 1:N_META], (HEAD_ROWS, LANES))
    ct = _cumsum_tokens(lft, _tri_matrices(tm, n_new), init, n_a, seq // tm, tm)

    o_p = _prompt_attention(q, kb, vb, ct, kb_m, vb_m, ct_m, n_batch, seq, tq)

    pt_flat = page_table.reshape(-1).astype(I32)
    lf_pages_t = jnp.swapaxes(cache_logf[layer], 1, 2)
    cpast = _past_cumsum(pt_flat, lf_pages_t, n_seq, n_pages)
    ctot = cpast[:, n_pages - 1, :, PAGE_SIZE - 1]
    cn = ct[:N_HEADS, n_p:].reshape(N_HEADS, n_seq, n_new)
    n_rows = N_HEADS * n_new
    cn_col = jnp.transpose(cn, (1, 0, 2)).reshape(n_seq, n_rows, 1)
    cn_rows = jnp.broadcast_to(jnp.transpose(cn, (1, 0, 2))[:, :, None, :],
                               (n_seq, N_HEADS, n_new, n_new)).reshape(n_seq, n_rows, n_new)
    cn_rows = jnp.pad(cn_rows, ((0, 0), (0, 0), (0, HEAD_ROWS - n_new)))
    ctot_col = jnp.repeat(ctot, n_new, axis=1)[:, :, None]
    k_s = k[n_p:].reshape(n_seq, n_new, ATT_WIDTH)
    v_s = v[n_p:].reshape(n_seq, n_new, ATT_WIDTH)
    q_s = q[n_p:].astype(F32).reshape(n_seq, n_new, ATT_WIDTH)
    ck = cache_k[layer].reshape(n_phys, PAGE_SIZE, ATT_WIDTH)
    cv = cache_v[layer].reshape(n_phys, PAGE_SIZE, ATT_WIDTH)
    o_s = _sample_attention(pt_flat, ck, cv, q_s, k_s, v_s, cn_rows, cn_col, ctot_col, cpast,
                            n_seq, n_pages, n_new, pps)
    o_s = o_s.reshape(n_s, ATT_WIDTH).astype(BF16)

    wdw = jnp.pad(w_dw[layer], ((0, 1), (0, 0)))
    bdw, lng, lnb = b_dw[layer][None, :], ln_g[layer][None, :], ln_b[layer][None, :]
    state0 = jnp.concatenate([jnp.zeros((HALO - N_META, d_conv), F32), u_m], axis=0)
    cs_p = _conv_prompt(u, state0, wdw, bdw, lng, lnb, n_batch, seq, tm)
    u_s = u[n_p:].reshape(n_seq, n_new, d_conv)
    cs_s, conv_sample = _conv_sample(state_conv[layer], u_s, wdw, bdw, lng, lnb, sb)
    cs_s = cs_s.reshape(n_s, d_conv)

    wr = jnp.pad(w_router[layer], ((0, 0), (0, LANES - N_EXPERTS)))
    wr1 = wr.astype(BF16)
    wr2 = (wr - wr1.astype(F32)).astype(BF16)
    wr12 = jnp.concatenate([wr1, wr2], axis=1)
    br = jnp.pad(b_router[layer], (0, LANES - N_EXPERTS), constant_values=NEG_BIG)[None, :]
    xm, hf, logits = _merge(xa, xb, cs_p, cs_s, o_p, o_s, g,
                            w_pw2[layer].astype(BF16), w_att_o[layer].astype(BF16), w_out[layer].astype(BF16),
                            g_ffn[layer][None, :], wr12, wr1, br, n_a, tm)

    n_all = n_p + n_s
    route, counts = _route(logits, t_route)
    cnt = counts[0, :N_EXPERTS].astype(I32)
    padded = (cnt + te - 1) // te * te
    pend = jnp.cumsum(padded)
    pstart = pend - padded
    pend0 = jnp.concatenate([jnp.zeros((1,), I32), pend]).astype(I32)
    idx = route[:, ROUTE_IDX:ROUTE_IDX + TOP_K].astype(I32)
    rank = route[:, ROUTE_RANK:ROUTE_RANK + TOP_K].astype(I32)
    dest = (pstart[idx] + rank).reshape(-1).astype(I32)
    n_blocks = -(-(n_all * TOP_K) // te) + N_EXPERTS
    n_used = (pend[-1] // te).astype(I32).reshape(1)
    blk = jnp.minimum(jnp.arange(n_blocks, dtype=I32), n_used[0] - 1) * te
    block_expert = jnp.minimum(jnp.searchsorted(pend, blk, side='right'), N_EXPERTS - 1).astype(I32)
    xs = _dispatch(pend0, dest, hf, n_blocks * te, t_disp, te)
    ys = _experts(block_expert, n_used, xs, w_gate_up[layer], b_gate_up[layer][:, None, :],
                  w_down[layer], b_down[layer][:, None, :], te)
    y_p, y_s = _combine(dest, ys, route, xm, g_final[None, :], n_p // t_comb, t_comb)

    def with_meta(real, m):
        real = real.reshape((n_batch, seq) + real.shape[1:])
        m = jnp.broadcast_to(m[None], (n_batch,) + m.shape)
        return jnp.concatenate([m, real], axis=1)

    hd = (N_HEADS, HEAD_DIM)
    k_prompt = with_meta(k[:n_p], k_m).reshape((1, n_batch, seq + N_META) + hd)
    v_prompt = with_meta(v[:n_p], v_m).reshape((1, n_batch, seq + N_META) + hd)
    logf_prompt = with_meta(lf[:n_p, :N_HEADS], lf_m[:, :N_HEADS])[None]
    n_st = CONV_WIDTH - 1
    conv_prompt = u[:n_p].reshape(n_batch, seq, d_conv)[:, seq - n_st:][None]
    k_sample = k_s.reshape((1, n_seq, n_new) + hd)
    v_sample = v_s.reshape((1, n_seq, n_new) + hd)
    logf_sample = lf[n_p:, :N_HEADS].reshape(1, n_seq, n_new, N_HEADS)
    return (y_p.reshape(n_batch, seq, d_model), y_s.reshape(n_seq, n_new, d_model),
            k_prompt, v_prompt, logf_prompt, conv_prompt, k_sample, v_sample, logf_sample, conv_sample[None])


def kernel(x_prompt, x_sample, cache_k, cache_v, cache_logf, state_conv, page_table, meta_tokens,
           g_mix, w_in, b_forget, w_dw, b_dw, ln_g, ln_b, w_pw2, w_att_o, w_out, g_ffn,
           w_router, b_router, w_gate_up, b_gate_up, w_down, b_down, g_final):
    return _forward(x_prompt, x_sample, cache_k, cache_v, cache_logf, state_conv, page_table, meta_tokens,
                    g_mix, w_in, b_forget, w_dw, b_dw, ln_g, ln_b, w_pw2, w_att_o, w_out, g_ffn,
                    w_router, b_router, w_gate_up, b_gate_up, w_down, b_down, g_final,
                    tm=512, tq=512, pps=16, sb=16, t_route=512, t_disp=256, te=256, t_comb=128)
```

```python
import functools

import jax
import jax.numpy as jnp
from jax import lax
from jax.experimental import pallas as pl
from jax.experimental.pallas import tpu as pltpu

F32 = jnp.float32
BF16 = jnp.bfloat16
I32 = jnp.int32

N_META = 16
N_HEADS = 8
HEAD_DIM = 64
ATT_WIDTH = N_HEADS * HEAD_DIM
CONV_WIDTH = 31
N_EXPERTS = 32
TOP_K = 4
SWIGLU_LIMIT = 7.0
SWIGLU_ALPHA = 1.702
NORM_EPS = 1e-5
SCALE = HEAD_DIM ** -0.5
PAGE_SIZE = 128

LANES = 128
HEAD_ROWS = 16
NEG_BIG = -1e30
VMEM_LIMIT = 56 * 1024 * 1024


def _cparams(n_axes):
    return pltpu.CompilerParams(dimension_semantics=("arbitrary",) * n_axes,
                                vmem_limit_bytes=VMEM_LIMIT)


def _dot(a, b):
    return jnp.dot(a, b, preferred_element_type=F32)


def _dot_nt(a, b):
    return lax.dot_general(a, b, (((1,), (1,)), ((), ())), preferred_element_type=F32)


def _split3(x):
    a = x.astype(BF16)
    r = x - a.astype(F32)
    b = r.astype(BF16)
    c = (r - b.astype(F32)).astype(BF16)
    return a, b, c


def _dot_exact_rhs01(x, m):
    a, b, c = _split3(x)
    return _dot(a, m) + _dot(b, m) + _dot(c, m)


def _dot_exact_lhs01(m, x):
    a, b, c = _split3(x)
    return _dot(m, a) + _dot(m, b) + _dot(m, c)


def _log_sigmoid(x):
    return jnp.minimum(x, 0.0) - jnp.log1p(jnp.exp(-jnp.abs(x)))


def _rms(x, g):
    ms = jnp.mean(x * x, axis=-1, keepdims=True)
    return x * lax.rsqrt(ms + NORM_EPS) * g


def _inproj_body(n_a, d_att, d_conv, d_model,
                 xa_ref, xb_ref, gmix_ref, w_ref, wft_ref, bfr_ref, bfc_ref,
                 q_ref, k_ref, v_ref, kb_ref, vb_ref, lf_ref, lft_ref, u_ref, g_ref):
    i = pl.program_id(0)
    x = jnp.where(i < n_a, xa_ref[...], xb_ref[...])
    hb = _rms(x, gmix_ref[...]).astype(BF16)
    o = 0
    q_ref[...] = (_dot(hb, w_ref[:, o:o + d_att]) * SCALE).astype(BF16)
    o += d_att
    kk = _dot(hb, w_ref[:, o:o + d_att])
    k_ref[...] = kk
    kb_ref[...] = kk.astype(BF16)
    o += d_att
    vv = _dot(hb, w_ref[:, o:o + d_att])
    v_ref[...] = vv
    vb_ref[...] = vv.astype(BF16)
    o += d_att
    ua = _dot(hb, w_ref[:, o:o + d_conv])
    ub = _dot(hb, w_ref[:, o + d_conv:o + 2 * d_conv])
    u_ref[...] = ua * jax.nn.sigmoid(ub)
    o += 2 * d_conv
    for c in range(0, 2 * d_model, 512):
        g_ref[:, c:c + 512] = jax.nn.sigmoid(_dot(hb, w_ref[:, o + c:o + c + 512])).astype(BF16)
    o += 2 * d_model
    lf_ref[...] = _log_sigmoid(_dot(hb, w_ref[:, o:o + LANES]) + bfr_ref[...])
    lft_ref[...] = _log_sigmoid(_dot_nt(wft_ref[...], hb) + bfc_ref[:, 0:1])


def _in_project(xa, xb, n_a, n_b, tm, gmix, w_all, wft, bfr, bfc):
    d_model = xa.shape[1]
    n_t = n_a + n_b
    n = n_t * tm
    d_att, d_conv = ATT_WIDTH, d_model // 2
    wcols = w_all.shape[1]
    row = lambda i: (i, 0)
    full = lambda i: (0, 0)
    out_shape = (
        jax.ShapeDtypeStruct((n, d_att), BF16),
        jax.ShapeDtypeStruct((n, d_att), F32),
        jax.ShapeDtypeStruct((n, d_att), F32),
        jax.ShapeDtypeStruct((n, d_att), BF16),
        jax.ShapeDtypeStruct((n, d_att), BF16),
        jax.ShapeDtypeStruct((n, LANES), F32),
        jax.ShapeDtypeStruct((HEAD_ROWS, n), F32),
        jax.ShapeDtypeStruct((n, d_conv), F32),
        jax.ShapeDtypeStruct((n, 2 * d_model), BF16),
    )
    out_specs = (
        pl.BlockSpec((tm, d_att), row), pl.BlockSpec((tm, d_att), row), pl.BlockSpec((tm, d_att), row),
        pl.BlockSpec((tm, d_att), row), pl.BlockSpec((tm, d_att), row),
        pl.BlockSpec((tm, LANES), row), pl.BlockSpec((HEAD_ROWS, tm), lambda i: (0, i)),
        pl.BlockSpec((tm, d_conv), row), pl.BlockSpec((tm, 2 * d_model), row),
    )
    in_specs = [
        pl.BlockSpec((tm, d_model), lambda i: (jnp.minimum(i, n_a - 1), 0)),
        pl.BlockSpec((tm, d_model), lambda i: (jnp.maximum(i - n_a, 0), 0)),
        pl.BlockSpec((1, d_model), full),
        pl.BlockSpec((d_model, wcols), full),
        pl.BlockSpec((HEAD_ROWS, d_model), full),
        pl.BlockSpec((1, LANES), full),
        pl.BlockSpec((HEAD_ROWS, LANES), full),
    ]
    return pl.pallas_call(
        functools.partial(_inproj_body, n_a, d_att, d_conv, d_model),
        grid=(n_t,), in_specs=in_specs, out_specs=out_specs, out_shape=out_shape,
        compiler_params=_cparams(1), name="in_project",
    )(xa, xb, gmix, w_all, wft, bfr, bfc)


def _cumsum_body(n_a, seg, lft_ref, tri_ref, init_ref, ct_ref, carry_ref):
    i = pl.program_id(0)
    is_a = i < n_a
    start = jnp.logical_and(is_a, i % seg == 0)
    carry = jnp.where(start, init_ref[...], carry_ref[...])
    carry = jnp.where(is_a, carry, 0.0)
    c = _dot_exact_rhs01(lft_ref[...], tri_ref[0]) + carry[:, 0:1]
    ct_ref[...] = c
    tm = c.shape[1]
    carry_ref[...] = jnp.broadcast_to(c[:, tm - 1:tm], carry_ref.shape)


def _cumsum_tokens(lft, tris, init, n_a, seg, tm):
    n = lft.shape[1]
    return pl.pallas_call(
        functools.partial(_cumsum_body, n_a, seg),
        grid=(n // tm,),
        in_specs=[pl.BlockSpec((HEAD_ROWS, tm), lambda i: (0, i)),
                  pl.BlockSpec((1, tm, tm), lambda i: (jnp.where(i < n_a, 0, 1), 0, 0)),
                  pl.BlockSpec((HEAD_ROWS, LANES), lambda i: (0, 0))],
        out_specs=pl.BlockSpec((HEAD_ROWS, tm), lambda i: (0, i)),
        out_shape=jax.ShapeDtypeStruct((HEAD_ROWS, n), F32),
        scratch_shapes=[pltpu.VMEM((HEAD_ROWS, LANES), F32)],
        compiler_params=_cparams(1), name="decay_cumsum",
    )(lft, tris, init)


def _attn_body(tq, q_ref, k_ref, v_ref, ct_ref, km_ref, vm_ref, ctm_ref, o_ref):
    hp = pl.program_id(1)
    i = pl.program_id(2)
    lane = lax.broadcasted_iota(I32, (1, LANES), 1)
    q2 = q_ref[...]
    km = km_ref[...]
    vm = vm_ref[...]
    rows = lax.broadcasted_iota(I32, (tq, tq), 0)
    cols = lax.broadcasted_iota(I32, (tq, tq), 1)
    outs = []
    for hh in range(2):
        in_head = (lane < HEAD_DIM) if hh == 0 else (lane >= HEAD_DIM)
        qh = jnp.where(in_head, q2, jnp.zeros_like(q2))
        head = 2 * hp + hh
        s = _dot_nt(qh, km) - ctm_ref[pl.ds(head, 1), :]
        m = jnp.max(s, axis=1, keepdims=True)
        p = jnp.exp(s - m)
        l = jnp.sum(p, axis=1, keepdims=True)
        acc = _dot(p.astype(BF16), vm)

        def tile(j, carry, diag):
            m, l, acc = carry
            k0 = pl.multiple_of(j * tq, tq)
            s = _dot_nt(qh, k_ref[pl.ds(k0, tq), :]) - ct_ref[pl.ds(head, 1), pl.ds(k0, tq)]
            if diag:
                s = jnp.where(cols <= rows, s, NEG_BIG)
            mn = jnp.maximum(m, jnp.max(s, axis=1, keepdims=True))
            a = jnp.exp(m - mn)
            p = jnp.exp(s - mn)
            l = a * l + jnp.sum(p, axis=1, keepdims=True)
            acc = a * acc + _dot(p.astype(BF16), v_ref[pl.ds(k0, tq), :])
            return mn, l, acc

        carry = lax.fori_loop(0, i, lambda j, c: tile(j, c, False), (m, l, acc))
        m, l, acc = tile(i, carry, True)
        outs.append(acc / l)
    o_ref[...] = jnp.where(lane < HEAD_DIM, outs[0], outs[1]).astype(o_ref.dtype)


def _prompt_attention(q, kb, vb, ct, kb_m, vb_m, ct_m, n_batch, seq, tq):
    n_hp = ATT_WIDTH // LANES
    n_q = seq // tq
    return pl.pallas_call(
        functools.partial(_attn_body, tq),
        grid=(n_batch, n_hp, n_q),
        in_specs=[pl.BlockSpec((tq, LANES), lambda b, h, i: (b * n_q + i, h)),
                  pl.BlockSpec((seq, LANES), lambda b, h, i: (b, h)),
                  pl.BlockSpec((seq, LANES), lambda b, h, i: (b, h)),
                  pl.BlockSpec((HEAD_ROWS, seq), lambda b, h, i: (0, b)),
                  pl.BlockSpec((N_META, LANES), lambda b, h, i: (0, h)),
                  pl.BlockSpec((N_META, LANES), lambda b, h, i: (0, h)),
                  pl.BlockSpec((HEAD_ROWS, N_META), lambda b, h, i: (0, 0))],
        out_specs=pl.BlockSpec((tq, LANES), lambda b, h, i: (b * n_q + i, h)),
        out_shape=jax.ShapeDtypeStruct((n_batch * seq, ATT_WIDTH), BF16),
        compiler_params=_cparams(3), name="prompt_attention",
    )(q, kb, vb, ct, kb_m, vb_m, ct_m)


def _past_cumsum_body(n_pages, pt_ref, *refs):
    page_refs = refs[:n_pages]
    out_ref = refs[n_pages]
    rows = n_pages * N_HEADS
    x = jnp.concatenate([r[0] for r in page_refs], axis=0)
    li = lax.broadcasted_iota(I32, (LANES, LANES), 0)
    lj = lax.broadcasted_iota(I32, (LANES, LANES), 1)
    same_head = (li % N_HEADS) == (lj % N_HEADS)
    pieces = _split3(x)
    upto = jnp.logical_and(same_head, li <= lj).astype(BF16)
    within = sum(_dot(p, upto) for p in pieces)
    totals = sum(_dot(p, same_head.astype(BF16)) for p in pieces)
    ri = lax.broadcasted_iota(I32, (rows, rows), 0)
    rj = lax.broadcasted_iota(I32, (rows, rows), 1)
    before = _dot_exact_lhs01((rj < ri).astype(BF16), totals)
    out_ref[0] = (within + before).reshape(n_pages, N_HEADS, LANES)


def _past_cumsum(page_table_flat, lf_pages_t, n_seq, n_pages):
    def pmap(p):
        return lambda b, pt: (pt[b * n_pages + p], 0, 0)
    grid_spec = pltpu.PrefetchScalarGridSpec(
        num_scalar_prefetch=1, grid=(n_seq,),
        in_specs=[pl.BlockSpec((1, N_HEADS, PAGE_SIZE), pmap(p)) for p in range(n_pages)],
        out_specs=pl.BlockSpec((1, n_pages, N_HEADS, PAGE_SIZE), lambda b, pt: (b, 0, 0, 0)))
    return pl.pallas_call(
        functools.partial(_past_cumsum_body, n_pages),
        grid_spec=grid_spec,
        out_shape=jax.ShapeDtypeStruct((n_seq, n_pages, N_HEADS, PAGE_SIZE), F32),
        compiler_params=_cparams(1), name="past_decay_cumsum",
    )(page_table_flat, *([lf_pages_t] * n_pages))


def _sample_attn_body(pps, n_new, pt_ref, q_ref, kn_ref, vn_ref, cnr_ref, cnc_ref, ctot_ref, cp_ref,
                      ck_ref, cv_ref, o_ref, m_ref, l_ref, acc_ref, kbuf, vbuf, sem):
    g = pl.program_id(1)
    n_g = pl.num_programs(1)
    step = pl.program_id(0) * n_g + g
    n_steps = pl.num_programs(0) * n_g
    slot = step % 2

    def page_copies(s, sl):
        out = []
        for p in range(pps):
            page = pt_ref[s * pps + p]
            out.append(pltpu.make_async_copy(ck_ref.at[page], kbuf.at[sl, p], sem.at[sl, 0]))
            out.append(pltpu.make_async_copy(cv_ref.at[page], vbuf.at[sl, p], sem.at[sl, 1]))
        return out

    @pl.when(step == 0)
    def _():
        for c in page_copies(step, slot):
            c.start()

    @pl.when(step + 1 < n_steps)
    def _():
        for c in page_copies(step + 1, 1 - slot):
            c.start()

    for c in page_copies(step, slot):
        c.wait()

    n_rows = N_HEADS * n_new
    n_cols = pps * PAGE_SIZE * N_HEADS
    qm = q_ref[0].astype(BF16)

    def own_head(shape, cols_head_minor):
        r = lax.broadcasted_iota(I32, shape, 0) // n_new
        c = lax.broadcasted_iota(I32, shape, 1)
        return r == (c % N_HEADS if cols_head_minor else c // n_new)

    @pl.when(g == 0)
    def _():
        m_ref[...] = jnp.full(m_ref.shape, NEG_BIG, F32)
        l_ref[...] = jnp.zeros(l_ref.shape, F32)
        acc_ref[...] = jnp.zeros(acc_ref.shape, F32)

    kb = kbuf[slot].reshape(n_cols, HEAD_DIM).astype(BF16)
    vb = vbuf[slot].reshape(n_cols, HEAD_DIM).astype(BF16)
    cp = cp_ref[0].reshape(pps * N_HEADS, LANES)
    bias = jnp.concatenate([jnp.broadcast_to(cp[j:j + 1, :], (n_rows, LANES)) for j in range(pps * N_HEADS)],
                           axis=1)
    s = jnp.where(own_head((n_rows, n_cols), True), _dot_nt(qm, kb) - bias, NEG_BIG)
    m_old = m_ref[:, 0:1]
    mn = jnp.maximum(m_old, jnp.max(s, axis=1, keepdims=True))
    a = jnp.exp(m_old - mn)
    p = jnp.exp(s - mn)
    l_new = a * l_ref[:, 0:1] + jnp.sum(p, axis=1, keepdims=True)
    acc_new = a * acc_ref[...] + _dot(p.astype(BF16), vb)
    m_ref[...] = jnp.broadcast_to(mn, m_ref.shape)
    l_ref[...] = jnp.broadcast_to(l_new, l_ref.shape)
    acc_ref[...] = acc_new

    @pl.when(g == n_g - 1)
    def _():
        cn_col = cnc_ref[0][:, 0:1]
        m_past = mn + ctot_ref[0][:, 0:1] + cn_col
        sn = _dot_nt(qm, kn_ref[0].astype(BF16)) + cn_col - cnr_ref[0]
        r_t = lax.broadcasted_iota(I32, sn.shape, 0) % n_new
        c_t = lax.broadcasted_iota(I32, sn.shape, 1) // N_HEADS
        sn = jnp.where(jnp.logical_and(own_head(sn.shape, True), c_t <= r_t), sn, NEG_BIG)
        m2 = jnp.maximum(m_past, jnp.max(sn, axis=1, keepdims=True))
        a2 = jnp.exp(m_past - m2)
        pn = jnp.exp(sn - m2)
        l2 = a2 * l_new + jnp.sum(pn, axis=1, keepdims=True)
        acc2 = a2 * acc_new + _dot(pn.astype(BF16), vn_ref[0].astype(BF16))
        o_ref[0] = acc2 / l2


def _sample_attention(page_table_flat, cache_k, cache_v, q_s, k_s, v_s, cn_row, cn_col, ctot_col, cpast,
                      n_seq, n_pages, n_new, pps):
    n_g = n_pages // pps
    n_rows = N_HEADS * n_new

    seq3 = lambda b, g, pt: (b, 0, 0)
    grid_spec = pltpu.PrefetchScalarGridSpec(
        num_scalar_prefetch=1, grid=(n_seq, n_g),
        in_specs=[pl.BlockSpec((1, n_rows, HEAD_DIM), seq3),
                  pl.BlockSpec((1, n_rows, HEAD_DIM), seq3),
                  pl.BlockSpec((1, n_rows, HEAD_DIM), seq3),
                  pl.BlockSpec((1, 1, n_rows), seq3),
                  pl.BlockSpec((1, n_rows, 1), seq3),
                  pl.BlockSpec((1, n_rows, 1), seq3),
                  pl.BlockSpec((1, pps, N_HEADS, PAGE_SIZE), lambda b, g, pt: (b, g, 0, 0)),
                  pl.BlockSpec(memory_space=pl.ANY), pl.BlockSpec(memory_space=pl.ANY)],
        out_specs=pl.BlockSpec((1, n_rows, HEAD_DIM), seq3),
        scratch_shapes=[pltpu.VMEM((n_rows, LANES), F32), pltpu.VMEM((n_rows, LANES), F32),
                        pltpu.VMEM((n_rows, HEAD_DIM), F32),
                        pltpu.VMEM((2, pps, PAGE_SIZE, N_HEADS, HEAD_DIM), F32),
                        pltpu.VMEM((2, pps, PAGE_SIZE, N_HEADS, HEAD_DIM), F32),
                        pltpu.SemaphoreType.DMA((2, 2))])
    return pl.pallas_call(
        functools.partial(_sample_attn_body, pps, n_new),
        grid_spec=grid_spec,
        out_shape=jax.ShapeDtypeStruct((n_seq, n_rows, HEAD_DIM), F32),
        compiler_params=_cparams(2), name="sample_attention",
    )(page_table_flat, q_s, k_s, v_s, cn_row, cn_col, ctot_col, cpast, cache_k, cache_v)


HALO = 32


def _conv_tail(y, bdw, lng, lnb):
    y = y + bdw
    mu = jnp.mean(y, axis=-1, keepdims=True)
    d = y - mu
    var = jnp.mean(d * d, axis=-1, keepdims=True)
    yn = d * lax.rsqrt(var + NORM_EPS) * lng + lnb
    return yn * jax.nn.sigmoid(yn)


def _conv_prompt_body(tm, u_ref, halo_ref, st_ref, wdw_ref, bdw_ref, lng_ref, lnb_ref, cs_ref, ctx_ref):
    i = pl.program_id(1)
    ctx_ref[0:HALO, :] = jnp.where(i == 0, st_ref[...], halo_ref[...])
    ctx_ref[HALO:HALO + tm, :] = u_ref[...]
    off = HALO - (CONV_WIDTH - 1)
    y = jnp.zeros((tm, u_ref.shape[1]), F32)
    for k in range(CONV_WIDTH):
        y = y + ctx_ref[off + k:off + k + tm, :] * wdw_ref[k:k + 1, :]
    cs_ref[...] = _conv_tail(y, bdw_ref[...], lng_ref[...], lnb_ref[...]).astype(cs_ref.dtype)


def _conv_prompt(u, state0, wdw, bdw, lng, lnb, n_batch, seq, tm):
    ch = u.shape[1]
    n_i = seq // tm
    hb = tm // HALO
    full = lambda b, i: (0, 0)
    return pl.pallas_call(
        functools.partial(_conv_prompt_body, tm),
        grid=(n_batch, n_i),
        in_specs=[pl.BlockSpec((tm, ch), lambda b, i: (b * n_i + i, 0)),
                  pl.BlockSpec((HALO, ch), lambda b, i: (jnp.maximum((b * n_i + i) * hb - 1, 0), 0)),
                  pl.BlockSpec((HALO, ch), full),
                  pl.BlockSpec((CONV_WIDTH + 1, ch), full),
                  pl.BlockSpec((1, ch), full), pl.BlockSpec((1, ch), full), pl.BlockSpec((1, ch), full)],
        out_specs=pl.BlockSpec((tm, ch), lambda b, i: (b * n_i + i, 0)),
        out_shape=jax.ShapeDtypeStruct((n_batch * seq, ch), BF16),
        scratch_shapes=[pltpu.VMEM((HALO + tm, ch), F32)],
        compiler_params=_cparams(2), name="conv_prompt",
    )(u, u, state0, wdw, bdw, lng, lnb)


def _conv_sample_body(n_new, st_ref, u_ref, wdw_ref, bdw_ref, lng_ref, lnb_ref, cs_ref, st_out_ref, ctx_ref):
    n_st = CONV_WIDTH - 1
    off = HALO - n_st
    ctx_ref[:, off:HALO, :] = st_ref[...]
    ctx_ref[:, HALO:HALO + n_new, :] = u_ref[...]
    sb, _, ch = u_ref.shape
    y = jnp.zeros((sb, n_new, ch), F32)
    for k in range(CONV_WIDTH):
        y = y + ctx_ref[:, off + k:off + k + n_new, :] * wdw_ref[k:k + 1, :]
    cs_ref[...] = _conv_tail(y, bdw_ref[...], lng_ref[...], lnb_ref[...]).astype(cs_ref.dtype)
    st_out_ref[...] = ctx_ref[:, off + n_new:off + n_new + n_st, :]


def _conv_sample(state, u, wdw, bdw, lng, lnb, sb):
    n_seq, n_new, ch = u.shape
    n_st = CONV_WIDTH - 1
    full = lambda i: (0, 0)
    blk = lambda i: (i, 0, 0)
    return pl.pallas_call(
        functools.partial(_conv_sample_body, n_new),
        grid=(n_seq // sb,),
        in_specs=[pl.BlockSpec((sb, n_st, ch), blk), pl.BlockSpec((sb, n_new, ch), blk),
                  pl.BlockSpec((CONV_WIDTH + 1, ch), full),
                  pl.BlockSpec((1, ch), full), pl.BlockSpec((1, ch), full), pl.BlockSpec((1, ch), full)],
        out_specs=(pl.BlockSpec((sb, n_new, ch), blk), pl.BlockSpec((sb, n_st, ch), blk)),
        out_shape=(jax.ShapeDtypeStruct((n_seq, n_new, ch), BF16),
                   jax.ShapeDtypeStruct((n_seq, n_st, ch), F32)),
        scratch_shapes=[pltpu.VMEM((sb, HALO + n_new, ch), F32)],
        compiler_params=_cparams(1), name="conv_sample",
    )(state, u, wdw, bdw, lng, lnb)


def _merge_body(n_a, d_model, xa_ref, xb_ref, csa_ref, csb_ref, oa_ref, ob_ref, g_ref,
                wpw_ref, wao_ref, wout_ref, gffn_ref, wr12_ref, wr1_ref, br_ref,
                xm_ref, hf_ref, lg_ref):
    i = pl.program_id(0)
    is_a = i < n_a
    x = jnp.where(is_a, xa_ref[...], xb_ref[...])
    cs = jnp.where(is_a, csa_ref[...], csb_ref[...])
    o = jnp.where(is_a, oa_ref[...], ob_ref[...])
    conv = _dot(cs, wpw_ref[...])
    att = _dot(o, wao_ref[...])
    mix = g_ref[:, 0:d_model].astype(F32) * conv + g_ref[:, d_model:2 * d_model].astype(F32) * att
    xm = x + _dot(mix.astype(BF16), wout_ref[...])
    xm_ref[...] = xm
    hf = _rms(xm, gffn_ref[...])
    hf_ref[...] = hf
    h1 = hf.astype(BF16)
    h2 = (hf - h1.astype(F32)).astype(BF16)
    big = _dot(h1, wr12_ref[...])
    lg_ref[...] = big[:, 0:LANES] + big[:, LANES:2 * LANES] + _dot(h2, wr1_ref[...]) + br_ref[...]


def _merge(xa, xb, csa, csb, oa, ob, g, wpw, wao, wout, gffn, wr12, wr1, br, n_a, tm):
    d_model = xa.shape[1]
    n = g.shape[0]
    n_t = n // tm
    row = lambda i: (i, 0)
    ra = lambda i: (jnp.minimum(i, n_a - 1), 0)
    rb = lambda i: (jnp.maximum(i - n_a, 0), 0)
    full = lambda i: (0, 0)
    dc, da = csa.shape[1], oa.shape[1]
    return pl.pallas_call(
        functools.partial(_merge_body, n_a, d_model),
        grid=(n_t,),
        in_specs=[pl.BlockSpec((tm, d_model), ra), pl.BlockSpec((tm, d_model), rb),
                  pl.BlockSpec((tm, dc), ra), pl.BlockSpec((tm, dc), rb),
                  pl.BlockSpec((tm, da), ra), pl.BlockSpec((tm, da), rb),
                  pl.BlockSpec((tm, 2 * d_model), row),
                  pl.BlockSpec((dc, d_model), full), pl.BlockSpec((da, d_model), full),
                  pl.BlockSpec((d_model, d_model), full), pl.BlockSpec((1, d_model), full),
                  pl.BlockSpec((d_model, 2 * LANES), full), pl.BlockSpec((d_model, LANES), full),
                  pl.BlockSpec((1, LANES), full)],
        out_specs=(pl.BlockSpec((tm, d_model), row), pl.BlockSpec((tm, d_model), row),
                   pl.BlockSpec((tm, LANES), row)),
        out_shape=(jax.ShapeDtypeStruct((n, d_model), F32), jax.ShapeDtypeStruct((n, d_model), F32),
                   jax.ShapeDtypeStruct((n, LANES), F32)),
        compiler_params=_cparams(1), name="merge_router",
    )(xa, xb, csa, csb, oa, ob, g, wpw, wao, wout, gffn, wr12, wr1, br)


ROUTE_IDX, ROUTE_RANK, ROUTE_GATE = 0, TOP_K, 2 * TOP_K


def _route_body(lg_ref, route_ref, cnt_ref, carry_ref):
    i = pl.program_id(0)

    @pl.when(i == 0)
    def _():
        carry_ref[...] = jnp.zeros(carry_ref.shape, F32)

    lg = lg_ref[...]
    tm = lg.shape[0]
    lane = lax.broadcasted_iota(I32, lg.shape, 1).astype(F32)
    vals, sels, idxs = [], [], []
    for _ in range(TOP_K):
        mk = jnp.max(lg, axis=1, keepdims=True)
        ik = jnp.min(jnp.where(lg == mk, lane, float(LANES)), axis=1, keepdims=True)
        sel = lane == ik
        vals.append(mk)
        idxs.append(ik)
        sels.append(sel)
        lg = jnp.where(sel, -3e38, lg)
    onehot = sum(s.astype(F32) for s in sels)
    es = [jnp.exp(v - vals[0]) for v in vals]
    den = sum(es)
    ri = lax.broadcasted_iota(I32, (tm, tm), 0)
    rj = lax.broadcasted_iota(I32, (tm, tm), 1)
    before = (rj < ri).astype(BF16)
    carry = carry_ref[0:1, :]
    cum = _dot(before, onehot.astype(BF16)) + carry
    route = jnp.zeros(lg.shape, F32)
    for k in range(TOP_K):
        rank = jnp.sum(jnp.where(sels[k], cum, 0.0), axis=1, keepdims=True)
        route = jnp.where(lane == ROUTE_IDX + k, idxs[k], route)
        route = jnp.where(lane == ROUTE_RANK + k, rank, route)
        route = jnp.where(lane == ROUTE_GATE + k, es[k] / den, route)
    route_ref[...] = route
    total = carry + jnp.sum(onehot, axis=0, keepdims=True)
    carry_ref[...] = jnp.broadcast_to(total, carry_ref.shape)
    cnt_ref[...] = jnp.broadcast_to(total, cnt_ref.shape)


def _route(logits, tm):
    n = logits.shape[0]
    return pl.pallas_call(
        _route_body,
        grid=(n // tm,),
        in_specs=[pl.BlockSpec((tm, LANES), lambda i: (i, 0))],
        out_specs=(pl.BlockSpec((tm, LANES), lambda i: (i, 0)), pl.BlockSpec((8, LANES), lambda i: (0, 0))),
        out_shape=(jax.ShapeDtypeStruct((n, LANES), F32), jax.ShapeDtypeStruct((8, LANES), F32)),
        scratch_shapes=[pltpu.VMEM((8, LANES), F32)],
        compiler_params=_cparams(1), name="route_topk",
    )(logits)


def _expert_body(te, d_exp, be_ref, nu_ref, rt_ref, hf_ref, wgu_ref, bgu_ref, wdn_ref, bdn_ref, y_ref,
                 wgu_b, wdn_b, xbuf, sem):
    i = pl.program_id(0)
    n_used = nu_ref[0]
    used = i < n_used
    slot = i % 2
    prev = be_ref[jnp.maximum(i - 1, 0)]
    fresh = jnp.logical_and(used, jnp.logical_or(i == 0, be_ref[i] != prev))

    def row_copy(block, sl, r):
        tok = rt_ref[block * te + r]
        return pltpu.make_async_copy(hf_ref.at[pl.ds(tok, 1), :], xbuf.at[sl, pl.ds(r, 1), :], sem.at[sl])

    def start_rows(block, sl):
        def body(r, c):
            row_copy(block, sl, r).start()
            return c
        lax.fori_loop(0, te, body, 0, unroll=8)

    @pl.when(i == 0)
    def _():
        start_rows(i, slot)

    @pl.when(i + 1 < n_used)
    def _():
        start_rows(i + 1, 1 - slot)

    @pl.when(fresh)
    def _():
        wgu_b[...] = wgu_ref[0].astype(BF16)
        wdn_b[...] = wdn_ref[0].astype(BF16)

    @pl.when(used)
    def _():
        def wait_row(r, c):
            row_copy(i, slot, r).wait()
            return c
        lax.fori_loop(0, te, wait_row, 0, unroll=8)
        xb = xbuf[slot].astype(BF16)
        gu = _dot(xb, wgu_b[...]) + bgu_ref[0]
        gate = jnp.minimum(gu[:, 0:d_exp], SWIGLU_LIMIT)
        lin = jnp.clip(gu[:, d_exp:2 * d_exp], -SWIGLU_LIMIT, SWIGLU_LIMIT)
        act = gate * jax.nn.sigmoid(SWIGLU_ALPHA * gate) * (lin + 1.0)
        y_ref[...] = _dot(act.astype(BF16), wdn_b[...]) + bdn_ref[0]

    @pl.when(jnp.logical_not(used))
    def _():
        y_ref[...] = jnp.zeros(y_ref.shape, F32)


def _experts(block_expert, n_used, row_tok, hf, wgu, bgu, wdn, bdn, te):
    d_model = hf.shape[1]
    n_rows = row_tok.shape[0]
    d_exp = wdn.shape[1]
    n_blocks = n_rows // te
    emap = lambda i, be, nu, rt: (be[i], 0, 0)
    grid_spec = pltpu.PrefetchScalarGridSpec(
        num_scalar_prefetch=3, grid=(n_blocks,),
        in_specs=[pl.BlockSpec(memory_space=pl.ANY),
                  pl.BlockSpec((1, d_model, 2 * d_exp), emap),
                  pl.BlockSpec((1, 1, 2 * d_exp), emap),
                  pl.BlockSpec((1, d_exp, d_model), emap),
                  pl.BlockSpec((1, 1, d_model), emap)],
        out_specs=pl.BlockSpec((te, d_model), lambda i, be, nu, rt: (i, 0)),
        scratch_shapes=[pltpu.VMEM((d_model, 2 * d_exp), BF16), pltpu.VMEM((d_exp, d_model), BF16),
                        pltpu.VMEM((2, te, d_model), F32), pltpu.SemaphoreType.DMA((2,))])
    return pl.pallas_call(
        functools.partial(_expert_body, te, d_exp),
        grid_spec=grid_spec,
        out_shape=jax.ShapeDtypeStruct((n_rows, d_model), F32),
        compiler_params=_cparams(1), name="moe_experts",
    )(block_expert, n_used, row_tok, hf, wgu, bgu, wdn, bdn)


def _combine_body(n_a, tm, dest_ref, ys_ref, route_ref, xm_ref, gfin_ref, ya_ref, yb_ref, buf_ref, sem):
    i = pl.program_id(0)

    def row_copy(r, k):
        d = dest_ref[r * TOP_K + k]
        return pltpu.make_async_copy(ys_ref.at[pl.ds(d, 1), :], buf_ref.at[k, pl.ds(r, 1), :], sem)

    def issue(r, c):
        for k in range(TOP_K):
            row_copy(r, k).start()
        return c

    def drain(r, c):
        for k in range(TOP_K):
            row_copy(r, k).wait()
        return c

    lax.fori_loop(0, tm, issue, 0)
    lax.fori_loop(0, tm, drain, 0)
    route = route_ref[...]
    y = jnp.zeros(xm_ref.shape, F32)
    for k in range(TOP_K):
        y = y + buf_ref[k] * route[:, ROUTE_GATE + k:ROUTE_GATE + k + 1]
    out = _rms(xm_ref[...] + y, gfin_ref[...])

    @pl.when(i < n_a)
    def _():
        ya_ref[...] = out

    @pl.when(i >= n_a)
    def _():
        yb_ref[...] = out


def _combine(dest_flat, ys, route, xm, gfin, n_a, tm):
    n, d_model = xm.shape
    n_t = n // tm
    n_b = n_t - n_a
    return pl.pallas_call(
        functools.partial(_combine_body, n_a, tm),
        grid=(n_t,),
        in_specs=[pl.BlockSpec((tm * TOP_K,), lambda i: (i,), memory_space=pltpu.SMEM),
                  pl.BlockSpec(memory_space=pl.ANY),
                  pl.BlockSpec((tm, LANES), lambda i: (i, 0)),
                  pl.BlockSpec((tm, d_model), lambda i: (i, 0)),
                  pl.BlockSpec((1, d_model), lambda i: (0, 0))],
        out_specs=(pl.BlockSpec((tm, d_model), lambda i: (jnp.minimum(i, n_a - 1), 0)),
                   pl.BlockSpec((tm, d_model), lambda i: (jnp.maximum(i - n_a, 0), 0))),
        out_shape=(jax.ShapeDtypeStruct((n_a * tm, d_model), F32),
                   jax.ShapeDtypeStruct((n_b * tm, d_model), F32)),
        scratch_shapes=[pltpu.VMEM((TOP_K, tm, d_model), F32), pltpu.SemaphoreType.DMA(())],
        compiler_params=_cparams(1), name="moe_combine",
    )(dest_flat, ys, route, xm, gfin)


def _prep_in_weights(w_in, b_forget, d_model):
    d_conv = d_model // 2
    s0, s1, s2 = ATT_WIDTH, 2 * ATT_WIDTH, 3 * ATT_WIDTH
    s3 = s2 + N_HEADS
    s4 = s3 + 2 * d_conv
    wf = w_in[:, s2:s3]
    wf_pad = jnp.pad(wf, ((0, 0), (0, LANES - N_HEADS)))
    w_all = jnp.concatenate([w_in[:, :s2], w_in[:, s3:s4], w_in[:, s4:], wf_pad], axis=1).astype(BF16)
    wft = jnp.pad(wf.T, ((0, HEAD_ROWS - N_HEADS), (0, 0))).astype(BF16)
    bfr = jnp.pad(b_forget, (0, LANES - N_HEADS))[None, :]
    bfc = jnp.broadcast_to(jnp.pad(b_forget, (0, HEAD_ROWS - N_HEADS))[:, None], (HEAD_ROWS, LANES))
    return w_all, wft, bfr, bfc


def _tri_matrices(tm, group):
    i = jnp.arange(tm)
    upper = i[:, None] <= i[None, :]
    same = (i[:, None] // group) == (i[None, :] // group)
    return jnp.stack([upper, jnp.logical_and(upper, same)]).astype(BF16)


def _forward(x_prompt, x_sample, cache_k, cache_v, cache_logf, state_conv, page_table, meta_tokens,
             g_mix, w_in, b_forget, w_dw, b_dw, ln_g, ln_b, w_pw2, w_att_o, w_out, g_ffn,
             w_router, b_router, w_gate_up, b_gate_up, w_down, b_down, g_final,
             tm, tq, pps, sb, t_route, te, t_comb):
    n_batch, seq, d_model = x_prompt.shape
    n_seq, n_new, _ = x_sample.shape
    n_pages = page_table.shape[1]
    n_phys = cache_k.shape[1]
    d_conv = d_model // 2
    n_p = n_batch * seq
    n_s = n_seq * n_new
    n_a = n_p // tm
    layer = 0

    xa = x_prompt.reshape(n_p, d_model)
    xb = x_sample.reshape(n_s, d_model)
    gmix = g_mix[layer][None, :]
    w_all, wft, bfr, bfc = _prep_in_weights(w_in[layer], b_forget[layer], d_model)

    meta = meta_tokens.astype(F32)
    (_, k_m, v_m, kb_m, vb_m, lf_m, lft_m, u_m, _) = _in_project(
        meta, meta, 1, 0, N_META, gmix, w_all, wft, bfr, bfc)
    (q, k, v, kb, vb, lf, lft, u, g) = _in_project(xa, xb, n_a, n_s // tm, tm, gmix, w_all, wft, bfr, bfc)

    tri_m = _tri_matrices(N_META, N_META)
    ct_m = _cumsum_tokens(lft_m, tri_m, jnp.zeros((HEAD_ROWS, LANES), F32), 1, 1, N_META)
    init = jnp.broadcast_to(ct_m[:, N_META - 1:N_META], (HEAD_ROWS, LANES))
    ct = _cumsum_tokens(lft, _tri_matrices(tm, n_new), init, n_a, seq // tm, tm)

    o_p = _prompt_attention(q, kb, vb, ct, kb_m, vb_m, ct_m, n_batch, seq, tq)

    pt_flat = page_table.reshape(-1).astype(I32)
    lf_pages = cache_logf[layer].reshape(n_phys, N_HEADS, PAGE_SIZE)
    cpast = _past_cumsum(pt_flat, lf_pages, n_seq, n_pages)
    ctot = cpast[:, n_pages - 1, N_HEADS - 1, PAGE_SIZE - N_HEADS:]
    n_rows = N_HEADS * n_new
    cn = jnp.transpose(ct[:N_HEADS, n_p:].reshape(N_HEADS, n_seq, n_new), (1, 2, 0))
    cn_row = cn.reshape(n_seq, 1, n_rows)
    cn_col = jnp.transpose(cn, (0, 2, 1)).reshape(n_seq, n_rows, 1)
    ctot_col = jnp.repeat(ctot, n_new, axis=1)[:, :, None]
    k_s = k[n_p:]
    v_s = v[n_p:]
    q_s = jnp.transpose(q[n_p:].astype(F32).reshape(n_seq, n_new, N_HEADS, HEAD_DIM), (0, 2, 1, 3))
    o_s = _sample_attention(pt_flat, cache_k[layer], cache_v[layer],
                            q_s.reshape(n_seq, n_rows, HEAD_DIM),
                            k_s.reshape(n_seq, n_rows, HEAD_DIM), v_s.reshape(n_seq, n_rows, HEAD_DIM),
                            cn_row, cn_col, ctot_col, cpast, n_seq, n_pages, n_new, pps)
    o_s = jnp.transpose(o_s.reshape(n_seq, N_HEADS, n_new, HEAD_DIM), (0, 2, 1, 3))
    o_s = o_s.reshape(n_s, ATT_WIDTH).astype(BF16)

    wdw = jnp.pad(w_dw[layer], ((0, 1), (0, 0)))
    bdw, lng, lnb = b_dw[layer][None, :], ln_g[layer][None, :], ln_b[layer][None, :]
    state0 = jnp.concatenate([jnp.zeros((HALO - N_META, d_conv), F32), u_m], axis=0)
    cs_p = _conv_prompt(u, state0, wdw, bdw, lng, lnb, n_batch, seq, tm)
    u_s = u[n_p:].reshape(n_seq, n_new, d_conv)
    cs_s, conv_sample = _conv_sample(state_conv[layer], u_s, wdw, bdw, lng, lnb, sb)
    cs_s = cs_s.reshape(n_s, d_conv)

    wr = jnp.pad(w_router[layer], ((0, 0), (0, LANES - N_EXPERTS)))
    wr1 = wr.astype(BF16)
    wr2 = (wr - wr1.astype(F32)).astype(BF16)
    wr12 = jnp.concatenate([wr1, wr2], axis=1)
    br = jnp.pad(b_router[layer], (0, LANES - N_EXPERTS), constant_values=NEG_BIG)[None, :]
    xm, hf, logits = _merge(xa, xb, cs_p, cs_s, o_p, o_s, g,
                            w_pw2[layer].astype(BF16), w_att_o[layer].astype(BF16), w_out[layer].astype(BF16),
                            g_ffn[layer][None, :], wr12, wr1, br, n_a, tm)

    n_all = n_p + n_s
    route, counts = _route(logits, t_route)
    cnt = counts[0, :N_EXPERTS].astype(I32)
    padded = (cnt + te - 1) // te * te
    pend = jnp.cumsum(padded)
    pstart = pend - padded
    idx = route[:, ROUTE_IDX:ROUTE_IDX + TOP_K].astype(I32)
    rank = route[:, ROUTE_RANK:ROUTE_RANK + TOP_K].astype(I32)
    dest = (pstart[idx] + rank).reshape(-1).astype(I32)
    n_blocks = -(-(n_all * TOP_K) // te) + N_EXPERTS
    n_used = (pend[-1] // te).astype(I32).reshape(1)
    blk = jnp.minimum(jnp.arange(n_blocks, dtype=I32), n_used[0] - 1) * te
    block_expert = jnp.minimum(jnp.sum(pend[None, :] <= blk[:, None], axis=1), N_EXPERTS - 1).astype(I32)
    row_tok = jnp.zeros((n_blocks * te,), I32).at[dest].set(jnp.arange(n_all * TOP_K, dtype=I32) // TOP_K)
    ys = _experts(block_expert, n_used, row_tok, hf, w_gate_up[layer], b_gate_up[layer][:, None, :],
                  w_down[layer], b_down[layer][:, None, :], te)
    y_p, y_s = _combine(dest, ys, route, xm, g_final[None, :], n_p // t_comb, t_comb)

    def with_meta(real, m):
        real = real.reshape((n_batch, seq) + real.shape[1:])
        m = jnp.broadcast_to(m[None], (n_batch,) + m.shape)
        return jnp.concatenate([m, real], axis=1)

    hd = (N_HEADS, HEAD_DIM)
    k_prompt = with_meta(k[:n_p], k_m).reshape((1, n_batch, seq + N_META) + hd)
    v_prompt = with_meta(v[:n_p], v_m).reshape((1, n_batch, seq + N_META) + hd)
    logf_prompt = with_meta(lf[:n_p, :N_HEADS], lf_m[:, :N_HEADS])[None]
    n_st = CONV_WIDTH - 1
    conv_prompt = u[:n_p].reshape(n_batch, seq, d_conv)[:, seq - n_st:][None]
    k_sample = k_s.reshape((1, n_seq, n_new) + hd)
    v_sample = v_s.reshape((1, n_seq, n_new) + hd)
    logf_sample = lf[n_p:, :N_HEADS].reshape(1, n_seq, n_new, N_HEADS)
    return (y_p.reshape(n_batch, seq, d_model), y_s.reshape(n_seq, n_new, d_model),
            k_prompt, v_prompt, logf_prompt, conv_prompt, k_sample, v_sample, logf_sample, conv_sample[None])


def kernel(x_prompt, x_sample, cache_k, cache_v, cache_logf, state_conv, page_table, meta_tokens,
           g_mix, w_in, b_forget, w_dw, b_dw, ln_g, ln_b, w_pw2, w_att_o, w_out, g_ffn,
           w_router, b_router, w_gate_up, b_gate_up, w_down, b_down, g_final):
    return _forward(x_prompt, x_sample, cache_k, cache_v, cache_logf, state_conv, page_table, meta_tokens,
                    g_mix, w_in, b_forget, w_dw, b_dw, ln_g, ln_b, w_pw2, w_att_o, w_out, g_ffn,
                    w_router, b_router, w_gate_up, b_gate_up, w_down, b_down, g_final,
                    tm=512, tq=512, pps=8, sb=16, t_route=512, te=256, t_comb=128)
```

```python
import functools

import jax
import jax.numpy as jnp
from jax import lax
from jax.experimental import pallas as pl
from jax.experimental.pallas import tpu as pltpu

F32 = jnp.float32
BF16 = jnp.bfloat16
I32 = jnp.int32

N_META = 16
N_HEADS = 8
HEAD_DIM = 64
ATT_WIDTH = N_HEADS * HEAD_DIM
CONV_WIDTH = 31
N_EXPERTS = 32
TOP_K = 4
SWIGLU_LIMIT = 7.0
SWIGLU_ALPHA = 1.702
NORM_EPS = 1e-5
SCALE = HEAD_DIM ** -0.5
PAGE_SIZE = 128

LANES = 128
HEAD_ROWS = 16
NEG_BIG = -1e30
VMEM_LIMIT = 56 * 1024 * 1024


def _cparams(n_axes):
    return pltpu.CompilerParams(dimension_semantics=("arbitrary",) * n_axes,
                                vmem_limit_bytes=VMEM_LIMIT)


def _dot(a, b):
    return jnp.dot(a, b, preferred_element_type=F32)


def _dot_nt(a, b):
    return lax.dot_general(a, b, (((1,), (1,)), ((), ())), preferred_element_type=F32)


def _split3(x):
    a = x.astype(BF16)
    r = x - a.astype(F32)
    b = r.astype(BF16)
    c = (r - b.astype(F32)).astype(BF16)
    return a, b, c


def _dot_exact_rhs01(x, m):
    a, b, c = _split3(x)
    return _dot(a, m) + _dot(b, m) + _dot(c, m)


def _dot_exact_lhs01(m, x):
    a, b, c = _split3(x)
    return _dot(m, a) + _dot(m, b) + _dot(m, c)


def _log_sigmoid(x):
    return jnp.minimum(x, 0.0) - jnp.log1p(jnp.exp(-jnp.abs(x)))


def _rms(x, g):
    ms = jnp.mean(x * x, axis=-1, keepdims=True)
    return x * lax.rsqrt(ms + NORM_EPS) * g


def _inproj_body(n_a, d_att, d_conv, d_model,
                 xa_ref, xb_ref, gmix_ref, w_ref, wft_ref, bfr_ref, bfc_ref,
                 q_ref, k_ref, v_ref, kb_ref, vb_ref, lf_ref, lft_ref, u_ref, g_ref):
    i = pl.program_id(0)
    x = jnp.where(i < n_a, xa_ref[...], xb_ref[...])
    hb = _rms(x, gmix_ref[...]).astype(BF16)
    o = 0
    q_ref[...] = (_dot(hb, w_ref[:, o:o + d_att]) * SCALE).astype(BF16)
    o += d_att
    kk = _dot(hb, w_ref[:, o:o + d_att])
    k_ref[...] = kk
    kb_ref[...] = kk.astype(BF16)
    o += d_att
    vv = _dot(hb, w_ref[:, o:o + d_att])
    v_ref[...] = vv
    vb_ref[...] = vv.astype(BF16)
    o += d_att
    ua = _dot(hb, w_ref[:, o:o + d_conv])
    ub = _dot(hb, w_ref[:, o + d_conv:o + 2 * d_conv])
    u_ref[...] = ua * jax.nn.sigmoid(ub)
    o += 2 * d_conv
    for c in range(0, 2 * d_model, 512):
        g_ref[:, c:c + 512] = jax.nn.sigmoid(_dot(hb, w_ref[:, o + c:o + c + 512])).astype(BF16)
    o += 2 * d_model
    lf_ref[...] = _log_sigmoid(_dot(hb, w_ref[:, o:o + LANES]) + bfr_ref[...])
    lft_ref[...] = _log_sigmoid(_dot_nt(wft_ref[...], hb) + bfc_ref[:, 0:1])


def _in_project(xa, xb, n_a, n_b, tm, gmix, w_all, wft, bfr, bfc):
    d_model = xa.shape[1]
    n_t = n_a + n_b
    n = n_t * tm
    d_att, d_conv = ATT_WIDTH, d_model // 2
    wcols = w_all.shape[1]
    row = lambda i: (i, 0)
    full = lambda i: (0, 0)
    out_shape = (
        jax.ShapeDtypeStruct((n, d_att), BF16),
        jax.ShapeDtypeStruct((n, d_att), F32),
        jax.ShapeDtypeStruct((n, d_att), F32),
        jax.ShapeDtypeStruct((n, d_att), BF16),
        jax.ShapeDtypeStruct((n, d_att), BF16),
        jax.ShapeDtypeStruct((n, LANES), F32),
        jax.ShapeDtypeStruct((HEAD_ROWS, n), F32),
        jax.ShapeDtypeStruct((n, d_conv), F32),
        jax.ShapeDtypeStruct((n, 2 * d_model), BF16),
    )
    out_specs = (
        pl.BlockSpec((tm, d_att), row), pl.BlockSpec((tm, d_att), row), pl.BlockSpec((tm, d_att), row),
        pl.BlockSpec((tm, d_att), row), pl.BlockSpec((tm, d_att), row),
        pl.BlockSpec((tm, LANES), row), pl.BlockSpec((HEAD_ROWS, tm), lambda i: (0, i)),
        pl.BlockSpec((tm, d_conv), row), pl.BlockSpec((tm, 2 * d_model), row),
    )
    in_specs = [
        pl.BlockSpec((tm, d_model), lambda i: (jnp.minimum(i, n_a - 1), 0)),
        pl.BlockSpec((tm, d_model), lambda i: (jnp.maximum(i - n_a, 0), 0)),
        pl.BlockSpec((1, d_model), full),
        pl.BlockSpec((d_model, wcols), full),
        pl.BlockSpec((HEAD_ROWS, d_model), full),
        pl.BlockSpec((1, LANES), full),
        pl.BlockSpec((HEAD_ROWS, LANES), full),
    ]
    return pl.pallas_call(
        functools.partial(_inproj_body, n_a, d_att, d_conv, d_model),
        grid=(n_t,), in_specs=in_specs, out_specs=out_specs, out_shape=out_shape,
        compiler_params=_cparams(1), name="in_project",
    )(xa, xb, gmix, w_all, wft, bfr, bfc)


def _cumsum_body(n_a, seg, lft_ref, tri_ref, init_ref, ct_ref, carry_ref):
    i = pl.program_id(0)
    is_a = i < n_a
    start = jnp.logical_and(is_a, i % seg == 0)
    carry = jnp.where(start, init_ref[...], carry_ref[...])
    carry = jnp.where(is_a, carry, 0.0)
    c = _dot_exact_rhs01(lft_ref[...], tri_ref[0]) + carry[:, 0:1]
    ct_ref[...] = c
    tm = c.shape[1]
    carry_ref[...] = jnp.broadcast_to(c[:, tm - 1:tm], carry_ref.shape)


def _cumsum_tokens(lft, tris, init, n_a, seg, tm):
    n = lft.shape[1]
    return pl.pallas_call(
        functools.partial(_cumsum_body, n_a, seg),
        grid=(n // tm,),
        in_specs=[pl.BlockSpec((HEAD_ROWS, tm), lambda i: (0, i)),
                  pl.BlockSpec((1, tm, tm), lambda i: (jnp.where(i < n_a, 0, 1), 0, 0)),
                  pl.BlockSpec((HEAD_ROWS, LANES), lambda i: (0, 0))],
        out_specs=pl.BlockSpec((HEAD_ROWS, tm), lambda i: (0, i)),
        out_shape=jax.ShapeDtypeStruct((HEAD_ROWS, n), F32),
        scratch_shapes=[pltpu.VMEM((HEAD_ROWS, LANES), F32)],
        compiler_params=_cparams(1), name="decay_cumsum",
    )(lft, tris, init)


def _attn_body(tq, q_ref, k_ref, v_ref, ct_ref, km_ref, vm_ref, ctm_ref, o_ref):
    hp = pl.program_id(1)
    i = pl.program_id(2)
    lane = lax.broadcasted_iota(I32, (1, LANES), 1)
    q2 = q_ref[...]
    km = km_ref[...]
    vm = vm_ref[...]
    rows = lax.broadcasted_iota(I32, (tq, tq), 0)
    cols = lax.broadcasted_iota(I32, (tq, tq), 1)
    outs = []
    for hh in range(2):
        in_head = (lane < HEAD_DIM) if hh == 0 else (lane >= HEAD_DIM)
        qh = jnp.where(in_head, q2, jnp.zeros_like(q2))
        head = 2 * hp + hh
        s = _dot_nt(qh, km) - ctm_ref[pl.ds(head, 1), :]
        m = jnp.max(s, axis=1, keepdims=True)
        p = jnp.exp(s - m)
        l = jnp.sum(p, axis=1, keepdims=True)
        acc = _dot(p.astype(BF16), vm)

        def tile(j, carry, diag):
            m, l, acc = carry
            k0 = pl.multiple_of(j * tq, tq)
            s = _dot_nt(qh, k_ref[pl.ds(k0, tq), :]) - ct_ref[pl.ds(head, 1), pl.ds(k0, tq)]
            if diag:
                s = jnp.where(cols <= rows, s, NEG_BIG)
            mn = jnp.maximum(m, jnp.max(s, axis=1, keepdims=True))
            a = jnp.exp(m - mn)
            p = jnp.exp(s - mn)
            l = a * l + jnp.sum(p, axis=1, keepdims=True)
            acc = a * acc + _dot(p.astype(BF16), v_ref[pl.ds(k0, tq), :])
            return mn, l, acc

        carry = lax.fori_loop(0, i, lambda j, c: tile(j, c, False), (m, l, acc))
        m, l, acc = tile(i, carry, True)
        outs.append(acc / l)
    o_ref[...] = jnp.where(lane < HEAD_DIM, outs[0], outs[1]).astype(o_ref.dtype)


def _prompt_attention(q, kb, vb, ct, kb_m, vb_m, ct_m, n_batch, seq, tq):
    n_hp = ATT_WIDTH // LANES
    n_q = seq // tq
    return pl.pallas_call(
        functools.partial(_attn_body, tq),
        grid=(n_batch, n_hp, n_q),
        in_specs=[pl.BlockSpec((tq, LANES), lambda b, h, i: (b * n_q + i, h)),
                  pl.BlockSpec((seq, LANES), lambda b, h, i: (b, h)),
                  pl.BlockSpec((seq, LANES), lambda b, h, i: (b, h)),
                  pl.BlockSpec((HEAD_ROWS, seq), lambda b, h, i: (0, b)),
                  pl.BlockSpec((N_META, LANES), lambda b, h, i: (0, h)),
                  pl.BlockSpec((N_META, LANES), lambda b, h, i: (0, h)),
                  pl.BlockSpec((HEAD_ROWS, N_META), lambda b, h, i: (0, 0))],
        out_specs=pl.BlockSpec((tq, LANES), lambda b, h, i: (b * n_q + i, h)),
        out_shape=jax.ShapeDtypeStruct((n_batch * seq, ATT_WIDTH), BF16),
        compiler_params=_cparams(3), name="prompt_attention",
    )(q, kb, vb, ct, kb_m, vb_m, ct_m)


def _past_cumsum_body(n_pages, pt_ref, *refs):
    page_refs = refs[:n_pages]
    out_ref = refs[n_pages]
    rows = n_pages * N_HEADS
    x = jnp.concatenate([r[0] for r in page_refs], axis=0)
    li = lax.broadcasted_iota(I32, (LANES, LANES), 0)
    lj = lax.broadcasted_iota(I32, (LANES, LANES), 1)
    within = _dot_exact_rhs01(x, (li <= lj).astype(BF16))
    ri = lax.broadcasted_iota(I32, (rows, rows), 0)
    rj = lax.broadcasted_iota(I32, (rows, rows), 1)
    earlier = jnp.logical_and(ri % N_HEADS == rj % N_HEADS, rj < ri).astype(BF16)
    totals = jnp.broadcast_to(within[:, LANES - 1:LANES], (rows, LANES))
    before = _dot_exact_lhs01(earlier, totals)
    out_ref[0] = (within + before).reshape(n_pages, N_HEADS, LANES)


def _past_cumsum(page_table_flat, lf_pages_t, n_seq, n_pages):
    def pmap(p):
        return lambda b, pt: (pt[b * n_pages + p], 0, 0)
    grid_spec = pltpu.PrefetchScalarGridSpec(
        num_scalar_prefetch=1, grid=(n_seq,),
        in_specs=[pl.BlockSpec((1, N_HEADS, PAGE_SIZE), pmap(p)) for p in range(n_pages)],
        out_specs=pl.BlockSpec((1, n_pages, N_HEADS, PAGE_SIZE), lambda b, pt: (b, 0, 0, 0)))
    return pl.pallas_call(
        functools.partial(_past_cumsum_body, n_pages),
        grid_spec=grid_spec,
        out_shape=jax.ShapeDtypeStruct((n_seq, n_pages, N_HEADS, PAGE_SIZE), F32),
        compiler_params=_cparams(1), name="past_decay_cumsum",
    )(page_table_flat, *([lf_pages_t] * n_pages))


def _sample_attn_body(pps, n_new, pt_ref, q_ref, kn_ref, vn_ref, cnr_ref, cnc_ref, ctot_ref, cp_ref,
                      ck_ref, cv_ref, o_ref, m_ref, l_ref, acc_ref, kbuf, vbuf, sem):
    g = pl.program_id(1)
    n_g = pl.num_programs(1)
    step = pl.program_id(0) * n_g + g
    n_steps = pl.num_programs(0) * n_g
    slot = step % 2

    def page_copies(s, sl):
        out = []
        for p in range(pps):
            page = pt_ref[s * pps + p]
            out.append(pltpu.make_async_copy(ck_ref.at[page], kbuf.at[sl, p], sem.at[sl, 0]))
            out.append(pltpu.make_async_copy(cv_ref.at[page], vbuf.at[sl, p], sem.at[sl, 1]))
        return out

    @pl.when(step == 0)
    def _():
        for c in page_copies(step, slot):
            c.start()

    @pl.when(step + 1 < n_steps)
    def _():
        for c in page_copies(step + 1, 1 - slot):
            c.start()

    for c in page_copies(step, slot):
        c.wait()

    n_rows = N_HEADS * n_new
    row = lax.broadcasted_iota(I32, (n_rows, ATT_WIDTH), 0)
    col = lax.broadcasted_iota(I32, (n_rows, ATT_WIDTH), 1)
    own_head = (row // n_new) == (col // HEAD_DIM)
    qbd = jnp.where(own_head, jnp.concatenate([q_ref[0]] * N_HEADS, axis=0), 0.0).astype(BF16)

    @pl.when(g == 0)
    def _():
        m_ref[...] = jnp.full(m_ref.shape, NEG_BIG, F32)
        l_ref[...] = jnp.zeros(l_ref.shape, F32)
        acc_ref[...] = jnp.zeros(acc_ref.shape, F32)

    kt = jnp.concatenate([kbuf[slot, p].reshape(ATT_WIDTH, PAGE_SIZE).astype(BF16) for p in range(pps)], axis=1)
    vt = jnp.concatenate([vbuf[slot, p].reshape(ATT_WIDTH, PAGE_SIZE).astype(BF16) for p in range(pps)], axis=1)
    cp = cp_ref[0]
    bias = jnp.concatenate(
        [jnp.broadcast_to(cp[p][:, None, :], (N_HEADS, n_new, PAGE_SIZE)).reshape(n_rows, PAGE_SIZE)
         for p in range(pps)], axis=1)
    s = _dot(qbd, kt) - bias
    m_old = m_ref[:, 0:1]
    mn = jnp.maximum(m_old, jnp.max(s, axis=1, keepdims=True))
    a = jnp.exp(m_old - mn)
    p = jnp.exp(s - mn)
    l_new = a * l_ref[:, 0:1] + jnp.sum(p, axis=1, keepdims=True)
    acc_new = a * acc_ref[...] + _dot_nt(p.astype(BF16), vt)
    m_ref[...] = jnp.broadcast_to(mn, m_ref.shape)
    l_ref[...] = jnp.broadcast_to(l_new, l_ref.shape)
    acc_ref[...] = acc_new

    @pl.when(g == n_g - 1)
    def _():
        cn_col = cnc_ref[0][:, 0:1]
        m_past = mn + ctot_ref[0][:, 0:1] + cn_col
        zpad = jnp.zeros((HEAD_ROWS - n_new, ATT_WIDTH), F32)
        kn = jnp.concatenate([kn_ref[0], zpad], axis=0).astype(BF16)
        vn = jnp.concatenate([vn_ref[0], zpad], axis=0).astype(BF16)
        sn = _dot_nt(qbd, kn) + cn_col - cnr_ref[0]
        r2 = lax.broadcasted_iota(I32, sn.shape, 0) % n_new
        c2 = lax.broadcasted_iota(I32, sn.shape, 1)
        sn = jnp.where(c2 <= r2, sn, NEG_BIG)
        m2 = jnp.maximum(m_past, jnp.max(sn, axis=1, keepdims=True))
        a2 = jnp.exp(m_past - m2)
        pn = jnp.exp(sn - m2)
        l2 = a2 * l_new + jnp.sum(pn, axis=1, keepdims=True)
        acc2 = a2 * acc_new + _dot(pn.astype(BF16), vn)
        o = jnp.where(own_head, acc2 / l2, 0.0)
        o_ref[0] = jnp.sum(o.reshape(N_HEADS, n_new, ATT_WIDTH), axis=0)


def _sample_attention(page_table_flat, cache_kt, cache_vt, q_s, k_s, v_s, cn_rows, cn_col, ctot_col, cpast,
                      n_seq, n_pages, n_new, pps):
    n_g = n_pages // pps
    n_rows = N_HEADS * n_new

    seq3 = lambda b, g, pt: (b, 0, 0)
    grid_spec = pltpu.PrefetchScalarGridSpec(
        num_scalar_prefetch=1, grid=(n_seq, n_g),
        in_specs=[pl.BlockSpec((1, n_new, ATT_WIDTH), seq3),
                  pl.BlockSpec((1, n_new, ATT_WIDTH), seq3),
                  pl.BlockSpec((1, n_new, ATT_WIDTH), seq3),
                  pl.BlockSpec((1, n_rows, HEAD_ROWS), seq3),
                  pl.BlockSpec((1, n_rows, 1), seq3),
                  pl.BlockSpec((1, n_rows, 1), seq3),
                  pl.BlockSpec((1, pps, N_HEADS, PAGE_SIZE), lambda b, g, pt: (b, g, 0, 0)),
                  pl.BlockSpec(memory_space=pl.ANY), pl.BlockSpec(memory_space=pl.ANY)],
        out_specs=pl.BlockSpec((1, n_new, ATT_WIDTH), seq3),
        scratch_shapes=[pltpu.VMEM((n_rows, LANES), F32), pltpu.VMEM((n_rows, LANES), F32),
                        pltpu.VMEM((n_rows, ATT_WIDTH), F32),
                        pltpu.VMEM((2, pps, N_HEADS, HEAD_DIM, PAGE_SIZE), F32),
                        pltpu.VMEM((2, pps, N_HEADS, HEAD_DIM, PAGE_SIZE), F32),
                        pltpu.SemaphoreType.DMA((2, 2))])
    return pl.pallas_call(
        functools.partial(_sample_attn_body, pps, n_new),
        grid_spec=grid_spec,
        out_shape=jax.ShapeDtypeStruct((n_seq, n_new, ATT_WIDTH), F32),
        compiler_params=_cparams(2), name="sample_attention",
    )(page_table_flat, q_s, k_s, v_s, cn_rows, cn_col, ctot_col, cpast, cache_kt, cache_vt)


HALO = 32


def _conv_tail(y, bdw, lng, lnb):
    y = y + bdw
    mu = jnp.mean(y, axis=-1, keepdims=True)
    d = y - mu
    var = jnp.mean(d * d, axis=-1, keepdims=True)
    yn = d * lax.rsqrt(var + NORM_EPS) * lng + lnb
    return yn * jax.nn.sigmoid(yn)


def _conv_prompt_body(tm, u_ref, halo_ref, st_ref, wdw_ref, bdw_ref, lng_ref, lnb_ref, cs_ref, ctx_ref):
    i = pl.program_id(1)
    ctx_ref[0:HALO, :] = jnp.where(i == 0, st_ref[...], halo_ref[...])
    ctx_ref[HALO:HALO + tm, :] = u_ref[...]
    off = HALO - (CONV_WIDTH - 1)
    y = jnp.zeros((tm, u_ref.shape[1]), F32)
    for k in range(CONV_WIDTH):
        y = y + ctx_ref[off + k:off + k + tm, :] * wdw_ref[k:k + 1, :]
    cs_ref[...] = _conv_tail(y, bdw_ref[...], lng_ref[...], lnb_ref[...]).astype(cs_ref.dtype)


def _conv_prompt(u, state0, wdw, bdw, lng, lnb, n_batch, seq, tm):
    ch = u.shape[1]
    n_i = seq // tm
    hb = tm // HALO
    full = lambda b, i: (0, 0)
    return pl.pallas_call(
        functools.partial(_conv_prompt_body, tm),
        grid=(n_batch, n_i),
        in_specs=[pl.BlockSpec((tm, ch), lambda b, i: (b * n_i + i, 0)),
                  pl.BlockSpec((HALO, ch), lambda b, i: (jnp.maximum((b * n_i + i) * hb - 1, 0), 0)),
                  pl.BlockSpec((HALO, ch), full),
                  pl.BlockSpec((CONV_WIDTH + 1, ch), full),
                  pl.BlockSpec((1, ch), full), pl.BlockSpec((1, ch), full), pl.BlockSpec((1, ch), full)],
        out_specs=pl.BlockSpec((tm, ch), lambda b, i: (b * n_i + i, 0)),
        out_shape=jax.ShapeDtypeStruct((n_batch * seq, ch), BF16),
        scratch_shapes=[pltpu.VMEM((HALO + tm, ch), F32)],
        compiler_params=_cparams(2), name="conv_prompt",
    )(u, u, state0, wdw, bdw, lng, lnb)


def _conv_sample_body(n_new, st_ref, u_ref, wdw_ref, bdw_ref, lng_ref, lnb_ref, cs_ref, st_out_ref, ctx_ref):
    n_st = CONV_WIDTH - 1
    off = HALO - n_st
    ctx_ref[:, off:HALO, :] = st_ref[...]
    ctx_ref[:, HALO:HALO + n_new, :] = u_ref[...]
    sb, _, ch = u_ref.shape
    y = jnp.zeros((sb, n_new, ch), F32)
    for k in range(CONV_WIDTH):
        y = y + ctx_ref[:, off + k:off + k + n_new, :] * wdw_ref[k:k + 1, :]
    cs_ref[...] = _conv_tail(y, bdw_ref[...], lng_ref[...], lnb_ref[...]).astype(cs_ref.dtype)
    st_out_ref[...] = ctx_ref[:, off + n_new:off + n_new + n_st, :]


def _conv_sample(state, u, wdw, bdw, lng, lnb, sb):
    n_seq, n_new, ch = u.shape
    n_st = CONV_WIDTH - 1
    full = lambda i: (0, 0)
    blk = lambda i: (i, 0, 0)
    return pl.pallas_call(
        functools.partial(_conv_sample_body, n_new),
        grid=(n_seq // sb,),
        in_specs=[pl.BlockSpec((sb, n_st, ch), blk), pl.BlockSpec((sb, n_new, ch), blk),
                  pl.BlockSpec((CONV_WIDTH + 1, ch), full),
                  pl.BlockSpec((1, ch), full), pl.BlockSpec((1, ch), full), pl.BlockSpec((1, ch), full)],
        out_specs=(pl.BlockSpec((sb, n_new, ch), blk), pl.BlockSpec((sb, n_st, ch), blk)),
        out_shape=(jax.ShapeDtypeStruct((n_seq, n_new, ch), BF16),
                   jax.ShapeDtypeStruct((n_seq, n_st, ch), F32)),
        scratch_shapes=[pltpu.VMEM((sb, HALO + n_new, ch), F32)],
        compiler_params=_cparams(1), name="conv_sample",
    )(state, u, wdw, bdw, lng, lnb)


def _merge_body(n_a, d_model, xa_ref, xb_ref, csa_ref, csb_ref, oa_ref, ob_ref, g_ref,
                wpw_ref, wao_ref, wout_ref, gffn_ref, wr12_ref, wr1_ref, br_ref,
                xm_ref, hf_ref, lg_ref):
    i = pl.program_id(0)
    is_a = i < n_a
    x = jnp.where(is_a, xa_ref[...], xb_ref[...])
    cs = jnp.where(is_a, csa_ref[...], csb_ref[...])
    o = jnp.where(is_a, oa_ref[...], ob_ref[...])
    conv = _dot(cs, wpw_ref[...])
    att = _dot(o, wao_ref[...])
    mix = g_ref[:, 0:d_model].astype(F32) * conv + g_ref[:, d_model:2 * d_model].astype(F32) * att
    xm = x + _dot(mix.astype(BF16), wout_ref[...])
    xm_ref[...] = xm
    hf = _rms(xm, gffn_ref[...])
    hf_ref[...] = hf
    h1 = hf.astype(BF16)
    h2 = (hf - h1.astype(F32)).astype(BF16)
    big = _dot(h1, wr12_ref[...])
    lg_ref[...] = big[:, 0:LANES] + big[:, LANES:2 * LANES] + _dot(h2, wr1_ref[...]) + br_ref[...]


def _merge(xa, xb, csa, csb, oa, ob, g, wpw, wao, wout, gffn, wr12, wr1, br, n_a, tm):
    d_model = xa.shape[1]
    n = g.shape[0]
    n_t = n // tm
    row = lambda i: (i, 0)
    ra = lambda i: (jnp.minimum(i, n_a - 1), 0)
    rb = lambda i: (jnp.maximum(i - n_a, 0), 0)
    full = lambda i: (0, 0)
    dc, da = csa.shape[1], oa.shape[1]
    return pl.pallas_call(
        functools.partial(_merge_body, n_a, d_model),
        grid=(n_t,),
        in_specs=[pl.BlockSpec((tm, d_model), ra), pl.BlockSpec((tm, d_model), rb),
                  pl.BlockSpec((tm, dc), ra), pl.BlockSpec((tm, dc), rb),
                  pl.BlockSpec((tm, da), ra), pl.BlockSpec((tm, da), rb),
                  pl.BlockSpec((tm, 2 * d_model), row),
                  pl.BlockSpec((dc, d_model), full), pl.BlockSpec((da, d_model), full),
                  pl.BlockSpec((d_model, d_model), full), pl.BlockSpec((1, d_model), full),
                  pl.BlockSpec((d_model, 2 * LANES), full), pl.BlockSpec((d_model, LANES), full),
                  pl.BlockSpec((1, LANES), full)],
        out_specs=(pl.BlockSpec((tm, d_model), row), pl.BlockSpec((tm, d_model), row),
                   pl.BlockSpec((tm, LANES), row)),
        out_shape=(jax.ShapeDtypeStruct((n, d_model), F32), jax.ShapeDtypeStruct((n, d_model), F32),
                   jax.ShapeDtypeStruct((n, LANES), F32)),
        compiler_params=_cparams(1), name="merge_router",
    )(xa, xb, csa, csb, oa, ob, g, wpw, wao, wout, gffn, wr12, wr1, br)


ROUTE_IDX, ROUTE_RANK, ROUTE_GATE = 0, TOP_K, 2 * TOP_K


def _route_body(lg_ref, route_ref, cnt_ref, carry_ref):
    i = pl.program_id(0)

    @pl.when(i == 0)
    def _():
        carry_ref[...] = jnp.zeros(carry_ref.shape, F32)

    lg = lg_ref[...]
    tm = lg.shape[0]
    lane = lax.broadcasted_iota(I32, lg.shape, 1).astype(F32)
    vals, sels, idxs = [], [], []
    for _ in range(TOP_K):
        mk = jnp.max(lg, axis=1, keepdims=True)
        ik = jnp.min(jnp.where(lg == mk, lane, float(LANES)), axis=1, keepdims=True)
        sel = lane == ik
        vals.append(mk)
        idxs.append(ik)
        sels.append(sel)
        lg = jnp.where(sel, -3e38, lg)
    onehot = sum(s.astype(F32) for s in sels)
    es = [jnp.exp(v - vals[0]) for v in vals]
    den = sum(es)
    ri = lax.broadcasted_iota(I32, (tm, tm), 0)
    rj = lax.broadcasted_iota(I32, (tm, tm), 1)
    before = (rj < ri).astype(BF16)
    carry = carry_ref[0:1, :]
    cum = _dot(before, onehot.astype(BF16)) + carry
    route = jnp.zeros(lg.shape, F32)
    for k in range(TOP_K):
        rank = jnp.sum(jnp.where(sels[k], cum, 0.0), axis=1, keepdims=True)
        route = jnp.where(lane == ROUTE_IDX + k, idxs[k], route)
        route = jnp.where(lane == ROUTE_RANK + k, rank, route)
        route = jnp.where(lane == ROUTE_GATE + k, es[k] / den, route)
    route_ref[...] = route
    total = carry + jnp.sum(onehot, axis=0, keepdims=True)
    carry_ref[...] = jnp.broadcast_to(total, carry_ref.shape)
    cnt_ref[...] = jnp.broadcast_to(total, cnt_ref.shape)


def _route(logits, tm):
    n = logits.shape[0]
    return pl.pallas_call(
        _route_body,
        grid=(n // tm,),
        in_specs=[pl.BlockSpec((tm, LANES), lambda i: (i, 0))],
        out_specs=(pl.BlockSpec((tm, LANES), lambda i: (i, 0)), pl.BlockSpec((8, LANES), lambda i: (0, 0))),
        out_shape=(jax.ShapeDtypeStruct((n, LANES), F32), jax.ShapeDtypeStruct((8, LANES), F32)),
        scratch_shapes=[pltpu.VMEM((8, LANES), F32)],
        compiler_params=_cparams(1), name="route_topk",
    )(logits)


def _expert_body(te, d_exp, be_ref, nu_ref, rt_ref, hf_ref, wgu_ref, bgu_ref, wdn_ref, bdn_ref, y_ref,
                 wgu_b, wdn_b, xbuf, sem):
    i = pl.program_id(0)
    n_used = nu_ref[0]
    used = i < n_used
    slot = i % 2
    prev = be_ref[jnp.maximum(i - 1, 0)]
    fresh = jnp.logical_and(used, jnp.logical_or(i == 0, be_ref[i] != prev))

    def row_copy(block, sl, r):
        tok = rt_ref[block * te + r]
        return pltpu.make_async_copy(hf_ref.at[pl.ds(tok, 1), :], xbuf.at[sl, pl.ds(r, 1), :], sem.at[sl])

    def start_rows(block, sl):
        def body(r, c):
            row_copy(block, sl, r).start()
            return c
        lax.fori_loop(0, te, body, 0, unroll=8)

    @pl.when(i == 0)
    def _():
        start_rows(i, slot)

    @pl.when(i + 1 < n_used)
    def _():
        start_rows(i + 1, 1 - slot)

    @pl.when(fresh)
    def _():
        wgu_b[...] = wgu_ref[0].astype(BF16)
        wdn_b[...] = wdn_ref[0].astype(BF16)

    @pl.when(used)
    def _():
        def wait_row(r, c):
            row_copy(i, slot, r).wait()
            return c
        lax.fori_loop(0, te, wait_row, 0, unroll=8)
        xb = xbuf[slot].astype(BF16)
        gu = _dot(xb, wgu_b[...]) + bgu_ref[0]
        gate = jnp.minimum(gu[:, 0:d_exp], SWIGLU_LIMIT)
        lin = jnp.clip(gu[:, d_exp:2 * d_exp], -SWIGLU_LIMIT, SWIGLU_LIMIT)
        act = gate * jax.nn.sigmoid(SWIGLU_ALPHA * gate) * (lin + 1.0)
        y_ref[...] = _dot(act.astype(BF16), wdn_b[...]) + bdn_ref[0]

    @pl.when(jnp.logical_not(used))
    def _():
        y_ref[...] = jnp.zeros(y_ref.shape, F32)


def _experts(block_expert, n_used, row_tok, hf, wgu, bgu, wdn, bdn, te):
    d_model = hf.shape[1]
    n_rows = row_tok.shape[0]
    d_exp = wdn.shape[1]
    n_blocks = n_rows // te
    emap = lambda i, be, nu, rt: (be[i], 0, 0)
    grid_spec = pltpu.PrefetchScalarGridSpec(
        num_scalar_prefetch=3, grid=(n_blocks,),
        in_specs=[pl.BlockSpec(memory_space=pl.ANY),
                  pl.BlockSpec((1, d_model, 2 * d_exp), emap),
                  pl.BlockSpec((1, 1, 2 * d_exp), emap),
                  pl.BlockSpec((1, d_exp, d_model), emap),
                  pl.BlockSpec((1, 1, d_model), emap)],
        out_specs=pl.BlockSpec((te, d_model), lambda i, be, nu, rt: (i, 0)),
        scratch_shapes=[pltpu.VMEM((d_model, 2 * d_exp), BF16), pltpu.VMEM((d_exp, d_model), BF16),
                        pltpu.VMEM((2, te, d_model), F32), pltpu.SemaphoreType.DMA((2,))])
    return pl.pallas_call(
        functools.partial(_expert_body, te, d_exp),
        grid_spec=grid_spec,
        out_shape=jax.ShapeDtypeStruct((n_rows, d_model), F32),
        compiler_params=_cparams(1), name="moe_experts",
    )(block_expert, n_used, row_tok, hf, wgu, bgu, wdn, bdn)


def _combine_body(n_a, tm, dest_ref, ys_ref, route_ref, xm_ref, gfin_ref, ya_ref, yb_ref, buf_ref, sem):
    i = pl.program_id(0)

    def row_copy(r, k):
        d = dest_ref[r * TOP_K + k]
        return pltpu.make_async_copy(ys_ref.at[pl.ds(d, 1), :], buf_ref.at[k, pl.ds(r, 1), :], sem)

    def issue(r, c):
        for k in range(TOP_K):
            row_copy(r, k).start()
        return c

    def drain(r, c):
        for k in range(TOP_K):
            row_copy(r, k).wait()
        return c

    lax.fori_loop(0, tm, issue, 0)
    lax.fori_loop(0, tm, drain, 0)
    route = route_ref[...]
    y = jnp.zeros(xm_ref.shape, F32)
    for k in range(TOP_K):
        y = y + buf_ref[k] * route[:, ROUTE_GATE + k:ROUTE_GATE + k + 1]
    out = _rms(xm_ref[...] + y, gfin_ref[...])

    @pl.when(i < n_a)
    def _():
        ya_ref[...] = out

    @pl.when(i >= n_a)
    def _():
        yb_ref[...] = out


def _combine(dest_flat, ys, route, xm, gfin, n_a, tm):
    n, d_model = xm.shape
    n_t = n // tm
    n_b = n_t - n_a
    return pl.pallas_call(
        functools.partial(_combine_body, n_a, tm),
        grid=(n_t,),
        in_specs=[pl.BlockSpec((tm * TOP_K,), lambda i: (i,), memory_space=pltpu.SMEM),
                  pl.BlockSpec(memory_space=pl.ANY),
                  pl.BlockSpec((tm, LANES), lambda i: (i, 0)),
                  pl.BlockSpec((tm, d_model), lambda i: (i, 0)),
                  pl.BlockSpec((1, d_model), lambda i: (0, 0))],
        out_specs=(pl.BlockSpec((tm, d_model), lambda i: (jnp.minimum(i, n_a - 1), 0)),
                   pl.BlockSpec((tm, d_model), lambda i: (jnp.maximum(i - n_a, 0), 0))),
        out_shape=(jax.ShapeDtypeStruct((n_a * tm, d_model), F32),
                   jax.ShapeDtypeStruct((n_b * tm, d_model), F32)),
        scratch_shapes=[pltpu.VMEM((TOP_K, tm, d_model), F32), pltpu.SemaphoreType.DMA(())],
        compiler_params=_cparams(1), name="moe_combine",
    )(dest_flat, ys, route, xm, gfin)


def _prep_in_weights(w_in, b_forget, d_model):
    d_conv = d_model // 2
    s0, s1, s2 = ATT_WIDTH, 2 * ATT_WIDTH, 3 * ATT_WIDTH
    s3 = s2 + N_HEADS
    s4 = s3 + 2 * d_conv
    wf = w_in[:, s2:s3]
    wf_pad = jnp.pad(wf, ((0, 0), (0, LANES - N_HEADS)))
    w_all = jnp.concatenate([w_in[:, :s2], w_in[:, s3:s4], w_in[:, s4:], wf_pad], axis=1).astype(BF16)
    wft = jnp.pad(wf.T, ((0, HEAD_ROWS - N_HEADS), (0, 0))).astype(BF16)
    bfr = jnp.pad(b_forget, (0, LANES - N_HEADS))[None, :]
    bfc = jnp.broadcast_to(jnp.pad(b_forget, (0, HEAD_ROWS - N_HEADS))[:, None], (HEAD_ROWS, LANES))
    return w_all, wft, bfr, bfc


def _tri_matrices(tm, group):
    i = jnp.arange(tm)
    upper = i[:, None] <= i[None, :]
    same = (i[:, None] // group) == (i[None, :] // group)
    return jnp.stack([upper, jnp.logical_and(upper, same)]).astype(BF16)


def _forward(x_prompt, x_sample, cache_k, cache_v, cache_logf, state_conv, page_table, meta_tokens,
             g_mix, w_in, b_forget, w_dw, b_dw, ln_g, ln_b, w_pw2, w_att_o, w_out, g_ffn,
             w_router, b_router, w_gate_up, b_gate_up, w_down, b_down, g_final,
             tm, tq, pps, sb, t_route, te, t_comb):
    n_batch, seq, d_model = x_prompt.shape
    n_seq, n_new, _ = x_sample.shape
    n_pages = page_table.shape[1]
    n_phys = cache_k.shape[1]
    d_conv = d_model // 2
    n_p = n_batch * seq
    n_s = n_seq * n_new
    n_a = n_p // tm
    layer = 0

    xa = x_prompt.reshape(n_p, d_model)
    xb = x_sample.reshape(n_s, d_model)
    gmix = g_mix[layer][None, :]
    w_all, wft, bfr, bfc = _prep_in_weights(w_in[layer], b_forget[layer], d_model)

    meta = meta_tokens.astype(F32)
    (_, k_m, v_m, kb_m, vb_m, lf_m, lft_m, u_m, _) = _in_project(
        meta, meta, 1, 0, N_META, gmix, w_all, wft, bfr, bfc)
    (q, k, v, kb, vb, lf, lft, u, g) = _in_project(xa, xb, n_a, n_s // tm, tm, gmix, w_all, wft, bfr, bfc)

    tri_m = _tri_matrices(N_META, N_META)
    ct_m = _cumsum_tokens(lft_m, tri_m, jnp.zeros((HEAD_ROWS, LANES), F32), 1, 1, N_META)
    init = jnp.broadcast_to(ct_m[:, N_META - 1:N_META], (HEAD_ROWS, LANES))
    ct = _cumsum_tokens(lft, _tri_matrices(tm, n_new), init, n_a, seq // tm, tm)

    o_p = _prompt_attention(q, kb, vb, ct, kb_m, vb_m, ct_m, n_batch, seq, tq)

    pt_flat = page_table.reshape(-1).astype(I32)
    lf_pages_t = jnp.swapaxes(cache_logf[layer], 1, 2)
    cache_kt = jnp.transpose(cache_k[layer], (0, 2, 3, 1))
    cache_vt = jnp.transpose(cache_v[layer], (0, 2, 3, 1))
    cpast = _past_cumsum(pt_flat, lf_pages_t, n_seq, n_pages)
    ctot = cpast[:, n_pages - 1, :, PAGE_SIZE - 1]
    n_rows = N_HEADS * n_new
    cn = jnp.transpose(ct[:N_HEADS, n_p:].reshape(N_HEADS, n_seq, n_new), (1, 0, 2))
    cn_col = cn.reshape(n_seq, n_rows, 1)
    cn_rows = jnp.broadcast_to(cn[:, :, None, :], (n_seq, N_HEADS, n_new, n_new)).reshape(n_seq, n_rows, n_new)
    cn_rows = jnp.pad(cn_rows, ((0, 0), (0, 0), (0, HEAD_ROWS - n_new)))
    ctot_col = jnp.repeat(ctot, n_new, axis=1)[:, :, None]
    k_s = k[n_p:].reshape(n_seq, n_new, ATT_WIDTH)
    v_s = v[n_p:].reshape(n_seq, n_new, ATT_WIDTH)
    q_s = q[n_p:].astype(F32).reshape(n_seq, n_new, ATT_WIDTH)
    o_s = _sample_attention(pt_flat, cache_kt, cache_vt, q_s, k_s, v_s, cn_rows, cn_col, ctot_col, cpast,
                            n_seq, n_pages, n_new, pps)
    o_s = o_s.reshape(n_s, ATT_WIDTH).astype(BF16)

    wdw = jnp.pad(w_dw[layer], ((0, 1), (0, 0)))
    bdw, lng, lnb = b_dw[layer][None, :], ln_g[layer][None, :], ln_b[layer][None, :]
    state0 = jnp.concatenate([jnp.zeros((HALO - N_META, d_conv), F32), u_m], axis=0)
    cs_p = _conv_prompt(u, state0, wdw, bdw, lng, lnb, n_batch, seq, tm)
    u_s = u[n_p:].reshape(n_seq, n_new, d_conv)
    cs_s, conv_sample = _conv_sample(state_conv[layer], u_s, wdw, bdw, lng, lnb, sb)
    cs_s = cs_s.reshape(n_s, d_conv)

    wr = jnp.pad(w_router[layer], ((0, 0), (0, LANES - N_EXPERTS)))
    wr1 = wr.astype(BF16)
    wr2 = (wr - wr1.astype(F32)).astype(BF16)
    wr12 = jnp.concatenate([wr1, wr2], axis=1)
    br = jnp.pad(b_router[layer], (0, LANES - N_EXPERTS), constant_values=NEG_BIG)[None, :]
    xm, hf, logits = _merge(xa, xb, cs_p, cs_s, o_p, o_s, g,
                            w_pw2[layer].astype(BF16), w_att_o[layer].astype(BF16), w_out[layer].astype(BF16),
                            g_ffn[layer][None, :], wr12, wr1, br, n_a, tm)

    n_all = n_p + n_s
    route, counts = _route(logits, t_route)
    cnt = counts[0, :N_EXPERTS].astype(I32)
    padded = (cnt + te - 1) // te * te
    pend = jnp.cumsum(padded)
    pstart = pend - padded
    idx = route[:, ROUTE_IDX:ROUTE_IDX + TOP_K].astype(I32)
    rank = route[:, ROUTE_RANK:ROUTE_RANK + TOP_K].astype(I32)
    dest = (pstart[idx] + rank).reshape(-1).astype(I32)
    n_blocks = -(-(n_all * TOP_K) // te) + N_EXPERTS
    n_used = (pend[-1] // te).astype(I32).reshape(1)
    blk = jnp.minimum(jnp.arange(n_blocks, dtype=I32), n_used[0] - 1) * te
    block_expert = jnp.minimum(jnp.sum(pend[None, :] <= blk[:, None], axis=1), N_EXPERTS - 1).astype(I32)
    row_tok = jnp.zeros((n_blocks * te,), I32).at[dest].set(jnp.arange(n_all * TOP_K, dtype=I32) // TOP_K)
    ys = _experts(block_expert, n_used, row_tok, hf, w_gate_up[layer], b_gate_up[layer][:, None, :],
                  w_down[layer], b_down[layer][:, None, :], te)
    y_p, y_s = _combine(dest, ys, route, xm, g_final[None, :], n_p // t_comb, t_comb)

    def with_meta(real, m):
        real = real.reshape((n_batch, seq) + real.shape[1:])
        m = jnp.broadcast_to(m[None], (n_batch,) + m.shape)
        return jnp.concatenate([m, real], axis=1)

    hd = (N_HEADS, HEAD_DIM)
    k_prompt = with_meta(k[:n_p], k_m).reshape((1, n_batch, seq + N_META) + hd)
    v_prompt = with_meta(v[:n_p], v_m).reshape((1, n_batch, seq + N_META) + hd)
    logf_prompt = with_meta(lf[:n_p, :N_HEADS], lf_m[:, :N_HEADS])[None]
    n_st = CONV_WIDTH - 1
    conv_prompt = u[:n_p].reshape(n_batch, seq, d_conv)[:, seq - n_st:][None]
    k_sample = k_s.reshape((1, n_seq, n_new) + hd)
    v_sample = v_s.reshape((1, n_seq, n_new) + hd)
    logf_sample = lf[n_p:, :N_HEADS].reshape(1, n_seq, n_new, N_HEADS)
    return (y_p.reshape(n_batch, seq, d_model), y_s.reshape(n_seq, n_new, d_model),
            k_prompt, v_prompt, logf_prompt, conv_prompt, k_sample, v_sample, logf_sample, conv_sample[None])


def kernel(x_prompt, x_sample, cache_k, cache_v, cache_logf, state_conv, page_table, meta_tokens,
           g_mix, w_in, b_forget, w_dw, b_dw, ln_g, ln_b, w_pw2, w_att_o, w_out, g_ffn,
           w_router, b_router, w_gate_up, b_gate_up, w_down, b_down, g_final):
    return _forward(x_prompt, x_sample, cache_k, cache_v, cache_logf, state_conv, page_table, meta_tokens,
                    g_mix, w_in, b_forget, w_dw, b_dw, ln_g, ln_b, w_pw2, w_att_o, w_out, g_ffn,
                    w_router, b_router, w_gate_up, b_gate_up, w_down, b_down, g_final,
                    tm=512, tq=512, pps=16, sb=16, t_route=512, te=256, t_comb=128)
```

```python
import functools

import jax
import jax.numpy as jnp
from jax import lax
from jax.experimental import pallas as pl
from jax.experimental.pallas import tpu as pltpu

F32 = jnp.float32
BF16 = jnp.bfloat16
I32 = jnp.int32

N_META = 16
N_HEADS = 8
HEAD_DIM = 64
ATT_WIDTH = N_HEADS * HEAD_DIM
CONV_WIDTH = 31
N_EXPERTS = 32
TOP_K = 4
SWIGLU_LIMIT = 7.0
SWIGLU_ALPHA = 1.702
NORM_EPS = 1e-5
SCALE = HEAD_DIM ** -0.5
PAGE_SIZE = 128

LANES = 128
HEAD_ROWS = 16
NEG_BIG = -1e30
VMEM_LIMIT = 56 * 1024 * 1024


def _cparams(n_axes):
    return pltpu.CompilerParams(dimension_semantics=("arbitrary",) * n_axes,
                                vmem_limit_bytes=VMEM_LIMIT)


def _dot(a, b):
    return jnp.dot(a, b, preferred_element_type=F32)


def _dot_nt(a, b):
    return lax.dot_general(a, b, (((1,), (1,)), ((), ())), preferred_element_type=F32)


def _split3(x):
    a = x.astype(BF16)
    r = x - a.astype(F32)
    b = r.astype(BF16)
    c = (r - b.astype(F32)).astype(BF16)
    return a, b, c


def _dot_exact_rhs01(x, m):
    a, b, c = _split3(x)
    return _dot(a, m) + _dot(b, m) + _dot(c, m)


def _dot_exact_lhs01(m, x):
    a, b, c = _split3(x)
    return _dot(m, a) + _dot(m, b) + _dot(m, c)


def _log_sigmoid(x):
    return jnp.minimum(x, 0.0) - jnp.log1p(jnp.exp(-jnp.abs(x)))


def _rms(x, g):
    ms = jnp.mean(x * x, axis=-1, keepdims=True)
    return x * lax.rsqrt(ms + NORM_EPS) * g


def _inproj_body(n_a, d_att, d_conv, d_model,
                 xa_ref, xb_ref, gmix_ref, w_ref, wft_ref, bfr_ref, bfc_ref,
                 q_ref, k_ref, v_ref, kb_ref, vb_ref, lf_ref, lft_ref, u_ref, g_ref):
    i = pl.program_id(0)
    x = jnp.where(i < n_a, xa_ref[...], xb_ref[...])
    hb = _rms(x, gmix_ref[...]).astype(BF16)
    o = 0
    q_ref[...] = (_dot(hb, w_ref[:, o:o + d_att]) * SCALE).astype(BF16)
    o += d_att
    kk = _dot(hb, w_ref[:, o:o + d_att])
    k_ref[...] = kk
    kb_ref[...] = kk.astype(BF16)
    o += d_att
    vv = _dot(hb, w_ref[:, o:o + d_att])
    v_ref[...] = vv
    vb_ref[...] = vv.astype(BF16)
    o += d_att
    ua = _dot(hb, w_ref[:, o:o + d_conv])
    ub = _dot(hb, w_ref[:, o + d_conv:o + 2 * d_conv])
    u_ref[...] = ua * jax.nn.sigmoid(ub)
    o += 2 * d_conv
    for c in range(0, 2 * d_model, 512):
        g_ref[:, c:c + 512] = jax.nn.sigmoid(_dot(hb, w_ref[:, o + c:o + c + 512])).astype(BF16)
    o += 2 * d_model
    lf_ref[...] = _log_sigmoid(_dot(hb, w_ref[:, o:o + LANES]) + bfr_ref[...])
    lft_ref[...] = _log_sigmoid(_dot_nt(wft_ref[...], hb) + bfc_ref[:, 0:1])


def _in_project(xa, xb, n_a, n_b, tm, gmix, w_all, wft, bfr, bfc):
    d_model = xa.shape[1]
    n_t = n_a + n_b
    n = n_t * tm
    d_att, d_conv = ATT_WIDTH, d_model // 2
    wcols = w_all.shape[1]
    row = lambda i: (i, 0)
    full = lambda i: (0, 0)
    out_shape = (
        jax.ShapeDtypeStruct((n, d_att), BF16),
        jax.ShapeDtypeStruct((n, d_att), F32),
        jax.ShapeDtypeStruct((n, d_att), F32),
        jax.ShapeDtypeStruct((n, d_att), BF16),
        jax.ShapeDtypeStruct((n, d_att), BF16),
        jax.ShapeDtypeStruct((n, LANES), F32),
        jax.ShapeDtypeStruct((HEAD_ROWS, n), F32),
        jax.ShapeDtypeStruct((n, d_conv), F32),
        jax.ShapeDtypeStruct((n, 2 * d_model), BF16),
    )
    out_specs = (
        pl.BlockSpec((tm, d_att), row), pl.BlockSpec((tm, d_att), row), pl.BlockSpec((tm, d_att), row),
        pl.BlockSpec((tm, d_att), row), pl.BlockSpec((tm, d_att), row),
        pl.BlockSpec((tm, LANES), row), pl.BlockSpec((HEAD_ROWS, tm), lambda i: (0, i)),
        pl.BlockSpec((tm, d_conv), row), pl.BlockSpec((tm, 2 * d_model), row),
    )
    in_specs = [
        pl.BlockSpec((tm, d_model), lambda i: (jnp.minimum(i, n_a - 1), 0)),
        pl.BlockSpec((tm, d_model), lambda i: (jnp.maximum(i - n_a, 0), 0)),
        pl.BlockSpec((1, d_model), full),
        pl.BlockSpec((d_model, wcols), full),
        pl.BlockSpec((HEAD_ROWS, d_model), full),
        pl.BlockSpec((1, LANES), full),
        pl.BlockSpec((HEAD_ROWS, LANES), full),
    ]
    return pl.pallas_call(
        functools.partial(_inproj_body, n_a, d_att, d_conv, d_model),
        grid=(n_t,), in_specs=in_specs, out_specs=out_specs, out_shape=out_shape,
        compiler_params=_cparams(1), name="in_project",
    )(xa, xb, gmix, w_all, wft, bfr, bfc)


def _cumsum_body(n_a, seg, lft_ref, tri_ref, init_ref, ct_ref, carry_ref):
    i = pl.program_id(0)
    is_a = i < n_a
    start = jnp.logical_and(is_a, i % seg == 0)
    carry = jnp.where(start, init_ref[...], carry_ref[...])
    carry = jnp.where(is_a, carry, 0.0)
    c = _dot_exact_rhs01(lft_ref[...], tri_ref[0]) + carry[:, 0:1]
    ct_ref[...] = c
    tm = c.shape[1]
    carry_ref[...] = jnp.broadcast_to(c[:, tm - 1:tm], carry_ref.shape)


def _cumsum_tokens(lft, tris, init, n_a, seg, tm):
    n = lft.shape[1]
    return pl.pallas_call(
        functools.partial(_cumsum_body, n_a, seg),
        grid=(n // tm,),
        in_specs=[pl.BlockSpec((HEAD_ROWS, tm), lambda i: (0, i)),
                  pl.BlockSpec((1, tm, tm), lambda i: (jnp.where(i < n_a, 0, 1), 0, 0)),
                  pl.BlockSpec((HEAD_ROWS, LANES), lambda i: (0, 0))],
        out_specs=pl.BlockSpec((HEAD_ROWS, tm), lambda i: (0, i)),
        out_shape=jax.ShapeDtypeStruct((HEAD_ROWS, n), F32),
        scratch_shapes=[pltpu.VMEM((HEAD_ROWS, LANES), F32)],
        compiler_params=_cparams(1), name="decay_cumsum",
    )(lft, tris, init)


def _attn_body(tq, q_ref, k_ref, v_ref, ct_ref, km_ref, vm_ref, ctm_ref, o_ref):
    hp = pl.program_id(1)
    i = pl.program_id(2)
    lane = lax.broadcasted_iota(I32, (1, LANES), 1)
    q2 = q_ref[...]
    km = km_ref[...]
    vm = vm_ref[...]
    rows = lax.broadcasted_iota(I32, (tq, tq), 0)
    cols = lax.broadcasted_iota(I32, (tq, tq), 1)
    outs = []
    for hh in range(2):
        in_head = (lane < HEAD_DIM) if hh == 0 else (lane >= HEAD_DIM)
        qh = jnp.where(in_head, q2, jnp.zeros_like(q2))
        head = 2 * hp + hh
        s = _dot_nt(qh, km) - ctm_ref[pl.ds(head, 1), :]
        m = jnp.max(s, axis=1, keepdims=True)
        p = jnp.exp(s - m)
        l = jnp.sum(p, axis=1, keepdims=True)
        acc = _dot(p.astype(BF16), vm)

        def tile(j, carry, diag):
            m, l, acc = carry
            k0 = pl.multiple_of(j * tq, tq)
            s = _dot_nt(qh, k_ref[pl.ds(k0, tq), :]) - ct_ref[pl.ds(head, 1), pl.ds(k0, tq)]
            if diag:
                s = jnp.where(cols <= rows, s, NEG_BIG)
            mn = jnp.maximum(m, jnp.max(s, axis=1, keepdims=True))
            a = jnp.exp(m - mn)
            p = jnp.exp(s - mn)
            l = a * l + jnp.sum(p, axis=1, keepdims=True)
            acc = a * acc + _dot(p.astype(BF16), v_ref[pl.ds(k0, tq), :])
            return mn, l, acc

        carry = lax.fori_loop(0, i, lambda j, c: tile(j, c, False), (m, l, acc))
        m, l, acc = tile(i, carry, True)
        outs.append(acc / l)
    o_ref[...] = jnp.where(lane < HEAD_DIM, outs[0], outs[1]).astype(o_ref.dtype)


def _prompt_attention(q, kb, vb, ct, kb_m, vb_m, ct_m, n_batch, seq, tq):
    n_hp = ATT_WIDTH // LANES
    n_q = seq // tq
    return pl.pallas_call(
        functools.partial(_attn_body, tq),
        grid=(n_batch, n_hp, n_q),
        in_specs=[pl.BlockSpec((tq, LANES), lambda b, h, i: (b * n_q + i, h)),
                  pl.BlockSpec((seq, LANES), lambda b, h, i: (b, h)),
                  pl.BlockSpec((seq, LANES), lambda b, h, i: (b, h)),
                  pl.BlockSpec((HEAD_ROWS, seq), lambda b, h, i: (0, b)),
                  pl.BlockSpec((N_META, LANES), lambda b, h, i: (0, h)),
                  pl.BlockSpec((N_META, LANES), lambda b, h, i: (0, h)),
                  pl.BlockSpec((HEAD_ROWS, N_META), lambda b, h, i: (0, 0))],
        out_specs=pl.BlockSpec((tq, LANES), lambda b, h, i: (b * n_q + i, h)),
        out_shape=jax.ShapeDtypeStruct((n_batch * seq, ATT_WIDTH), BF16),
        compiler_params=_cparams(3), name="prompt_attention",
    )(q, kb, vb, ct, kb_m, vb_m, ct_m)


def _page_cumsum_body(x_ref, o_ref):
    li = lax.broadcasted_iota(I32, (LANES, LANES), 0)
    lj = lax.broadcasted_iota(I32, (LANES, LANES), 1)
    o_ref[...] = _dot_exact_rhs01(x_ref[...], (li <= lj).astype(BF16))


def _page_cumsum(lf_pages_t, tp):
    n_phys = lf_pages_t.shape[0]
    x = lf_pages_t.reshape(n_phys * N_HEADS, PAGE_SIZE)
    rows = tp * N_HEADS
    out = pl.pallas_call(
        _page_cumsum_body,
        grid=(n_phys // tp,),
        in_specs=[pl.BlockSpec((rows, PAGE_SIZE), lambda i: (i, 0))],
        out_specs=pl.BlockSpec((rows, PAGE_SIZE), lambda i: (i, 0)),
        out_shape=jax.ShapeDtypeStruct(x.shape, F32),
        compiler_params=_cparams(1), name="page_decay_cumsum",
    )(x)
    return out.reshape(n_phys, N_HEADS, PAGE_SIZE)


def _sample_attn_body(pps, n_new, pt_ref, q_ref, kn_ref, vn_ref, cnr_ref, cnc_ref,
                      ck_ref, cv_ref, cw_ref, o_ref, m_ref, l_ref, acc_ref, run_ref, kbuf, vbuf, cbuf, sem):
    g = pl.program_id(1)
    n_g = pl.num_programs(1)
    step = pl.program_id(0) * n_g + g
    n_steps = pl.num_programs(0) * n_g
    slot = step % 2

    def page_copies(s, sl):
        out = []
        for p in range(pps):
            page = pt_ref[s * pps + p]
            out.append(pltpu.make_async_copy(ck_ref.at[page], kbuf.at[sl, p], sem.at[sl, 0]))
            out.append(pltpu.make_async_copy(cv_ref.at[page], vbuf.at[sl, p], sem.at[sl, 1]))
            out.append(pltpu.make_async_copy(cw_ref.at[page], cbuf.at[sl, p], sem.at[sl, 2]))
        return out

    @pl.when(step == 0)
    def _():
        for c in page_copies(step, slot):
            c.start()

    @pl.when(step + 1 < n_steps)
    def _():
        for c in page_copies(step + 1, 1 - slot):
            c.start()

    for c in page_copies(step, slot):
        c.wait()

    n_rows = N_HEADS * n_new
    row = lax.broadcasted_iota(I32, (n_rows, ATT_WIDTH), 0)
    col = lax.broadcasted_iota(I32, (n_rows, ATT_WIDTH), 1)
    own_head = (row // n_new) == (col // HEAD_DIM)
    qbd = jnp.where(own_head, jnp.concatenate([q_ref[0]] * N_HEADS, axis=0), 0.0).astype(BF16)

    @pl.when(g == 0)
    def _():
        m_ref[...] = jnp.full(m_ref.shape, NEG_BIG, F32)
        l_ref[...] = jnp.zeros(l_ref.shape, F32)
        acc_ref[...] = jnp.zeros(acc_ref.shape, F32)
        run_ref[...] = jnp.zeros(run_ref.shape, F32)

    def per_row(x):
        return jnp.broadcast_to(x[:, None, :], (N_HEADS, n_new, PAGE_SIZE)).reshape(n_rows, PAGE_SIZE)

    kt = jnp.concatenate([kbuf[slot, p].reshape(ATT_WIDTH, PAGE_SIZE).astype(BF16) for p in range(pps)], axis=1)
    vt = jnp.concatenate([vbuf[slot, p].reshape(ATT_WIDTH, PAGE_SIZE).astype(BF16) for p in range(pps)], axis=1)
    run = run_ref[...]
    biases = []
    for p in range(pps):
        cw = cbuf[slot, p]
        biases.append(per_row(cw + run))
        run = run + jnp.broadcast_to(cw[:, PAGE_SIZE - 1:PAGE_SIZE], run.shape)
    run_ref[...] = run
    bias = jnp.concatenate(biases, axis=1)
    s = _dot(qbd, kt) - bias
    m_old = m_ref[:, 0:1]
    mn = jnp.maximum(m_old, jnp.max(s, axis=1, keepdims=True))
    a = jnp.exp(m_old - mn)
    p = jnp.exp(s - mn)
    l_new = a * l_ref[:, 0:1] + jnp.sum(p, axis=1, keepdims=True)
    acc_new = a * acc_ref[...] + _dot_nt(p.astype(BF16), vt)
    m_ref[...] = jnp.broadcast_to(mn, m_ref.shape)
    l_ref[...] = jnp.broadcast_to(l_new, l_ref.shape)
    acc_ref[...] = acc_new

    @pl.when(g == n_g - 1)
    def _():
        cn_col = cnc_ref[0][:, 0:1]
        m_past = mn + per_row(run)[:, 0:1] + cn_col
        zpad = jnp.zeros((HEAD_ROWS - n_new, ATT_WIDTH), F32)
        kn = jnp.concatenate([kn_ref[0], zpad], axis=0).astype(BF16)
        vn = jnp.concatenate([vn_ref[0], zpad], axis=0).astype(BF16)
        sn = _dot_nt(qbd, kn) + cn_col - cnr_ref[0]
        r2 = lax.broadcasted_iota(I32, sn.shape, 0) % n_new
        c2 = lax.broadcasted_iota(I32, sn.shape, 1)
        sn = jnp.where(c2 <= r2, sn, NEG_BIG)
        m2 = jnp.maximum(m_past, jnp.max(sn, axis=1, keepdims=True))
        a2 = jnp.exp(m_past - m2)
        pn = jnp.exp(sn - m2)
        l2 = a2 * l_new + jnp.sum(pn, axis=1, keepdims=True)
        acc2 = a2 * acc_new + _dot(pn.astype(BF16), vn)
        o = jnp.where(own_head, acc2 / l2, 0.0)
        o_ref[0] = jnp.sum(o.reshape(N_HEADS, n_new, ATT_WIDTH), axis=0)


def _sample_attention(page_table_flat, cache_kt, cache_vt, page_cw, q_s, k_s, v_s, cn_rows, cn_col,
                      n_seq, n_pages, n_new, pps):
    n_g = n_pages // pps
    n_rows = N_HEADS * n_new

    seq3 = lambda b, g, pt: (b, 0, 0)
    grid_spec = pltpu.PrefetchScalarGridSpec(
        num_scalar_prefetch=1, grid=(n_seq, n_g),
        in_specs=[pl.BlockSpec((1, n_new, ATT_WIDTH), seq3),
                  pl.BlockSpec((1, n_new, ATT_WIDTH), seq3),
                  pl.BlockSpec((1, n_new, ATT_WIDTH), seq3),
                  pl.BlockSpec((1, n_rows, HEAD_ROWS), seq3),
                  pl.BlockSpec((1, n_rows, 1), seq3),
                  pl.BlockSpec(memory_space=pl.ANY), pl.BlockSpec(memory_space=pl.ANY),
                  pl.BlockSpec(memory_space=pl.ANY)],
        out_specs=pl.BlockSpec((1, n_new, ATT_WIDTH), seq3),
        scratch_shapes=[pltpu.VMEM((n_rows, LANES), F32), pltpu.VMEM((n_rows, LANES), F32),
                        pltpu.VMEM((n_rows, ATT_WIDTH), F32),
                        pltpu.VMEM((N_HEADS, PAGE_SIZE), F32),
                        pltpu.VMEM((2, pps, N_HEADS, HEAD_DIM, PAGE_SIZE), F32),
                        pltpu.VMEM((2, pps, N_HEADS, HEAD_DIM, PAGE_SIZE), F32),
                        pltpu.VMEM((2, pps, N_HEADS, PAGE_SIZE), F32),
                        pltpu.SemaphoreType.DMA((2, 3))])
    return pl.pallas_call(
        functools.partial(_sample_attn_body, pps, n_new),
        grid_spec=grid_spec,
        out_shape=jax.ShapeDtypeStruct((n_seq, n_new, ATT_WIDTH), F32),
        compiler_params=_cparams(2), name="sample_attention",
    )(page_table_flat, q_s, k_s, v_s, cn_rows, cn_col, cache_kt, cache_vt, page_cw)


HALO = 32


def _conv_tail(y, bdw, lng, lnb):
    y = y + bdw
    mu = jnp.mean(y, axis=-1, keepdims=True)
    d = y - mu
    var = jnp.mean(d * d, axis=-1, keepdims=True)
    yn = d * lax.rsqrt(var + NORM_EPS) * lng + lnb
    return yn * jax.nn.sigmoid(yn)


def _conv_prompt_body(tm, u_ref, halo_ref, st_ref, wdw_ref, bdw_ref, lng_ref, lnb_ref, cs_ref, ctx_ref):
    i = pl.program_id(1)
    ctx_ref[0:HALO, :] = jnp.where(i == 0, st_ref[...], halo_ref[...])
    ctx_ref[HALO:HALO + tm, :] = u_ref[...]
    off = HALO - (CONV_WIDTH - 1)
    y = jnp.zeros((tm, u_ref.shape[1]), F32)
    for k in range(CONV_WIDTH):
        y = y + ctx_ref[off + k:off + k + tm, :] * wdw_ref[k:k + 1, :]
    cs_ref[...] = _conv_tail(y, bdw_ref[...], lng_ref[...], lnb_ref[...]).astype(cs_ref.dtype)


def _conv_prompt(u, state0, wdw, bdw, lng, lnb, n_batch, seq, tm):
    ch = u.shape[1]
    n_i = seq // tm
    hb = tm // HALO
    full = lambda b, i: (0, 0)
    return pl.pallas_call(
        functools.partial(_conv_prompt_body, tm),
        grid=(n_batch, n_i),
        in_specs=[pl.BlockSpec((tm, ch), lambda b, i: (b * n_i + i, 0)),
                  pl.BlockSpec((HALO, ch), lambda b, i: (jnp.maximum((b * n_i + i) * hb - 1, 0), 0)),
                  pl.BlockSpec((HALO, ch), full),
                  pl.BlockSpec((CONV_WIDTH + 1, ch), full),
                  pl.BlockSpec((1, ch), full), pl.BlockSpec((1, ch), full), pl.BlockSpec((1, ch), full)],
        out_specs=pl.BlockSpec((tm, ch), lambda b, i: (b * n_i + i, 0)),
        out_shape=jax.ShapeDtypeStruct((n_batch * seq, ch), BF16),
        scratch_shapes=[pltpu.VMEM((HALO + tm, ch), F32)],
        compiler_params=_cparams(2), name="conv_prompt",
    )(u, u, state0, wdw, bdw, lng, lnb)


def _conv_sample_body(n_new, st_ref, u_ref, wdw_ref, bdw_ref, lng_ref, lnb_ref, cs_ref, st_out_ref, ctx_ref):
    n_st = CONV_WIDTH - 1
    off = HALO - n_st
    ctx_ref[:, off:HALO, :] = st_ref[...]
    ctx_ref[:, HALO:HALO + n_new, :] = u_ref[...]
    sb, _, ch = u_ref.shape
    y = jnp.zeros((sb, n_new, ch), F32)
    for k in range(CONV_WIDTH):
        y = y + ctx_ref[:, off + k:off + k + n_new, :] * wdw_ref[k:k + 1, :]
    cs_ref[...] = _conv_tail(y, bdw_ref[...], lng_ref[...], lnb_ref[...]).astype(cs_ref.dtype)
    st_out_ref[...] = ctx_ref[:, off + n_new:off + n_new + n_st, :]


def _conv_sample(state, u, wdw, bdw, lng, lnb, sb):
    n_seq, n_new, ch = u.shape
    n_st = CONV_WIDTH - 1
    full = lambda i: (0, 0)
    blk = lambda i: (i, 0, 0)
    return pl.pallas_call(
        functools.partial(_conv_sample_body, n_new),
        grid=(n_seq // sb,),
        in_specs=[pl.BlockSpec((sb, n_st, ch), blk), pl.BlockSpec((sb, n_new, ch), blk),
                  pl.BlockSpec((CONV_WIDTH + 1, ch), full),
                  pl.BlockSpec((1, ch), full), pl.BlockSpec((1, ch), full), pl.BlockSpec((1, ch), full)],
        out_specs=(pl.BlockSpec((sb, n_new, ch), blk), pl.BlockSpec((sb, n_st, ch), blk)),
        out_shape=(jax.ShapeDtypeStruct((n_seq, n_new, ch), BF16),
                   jax.ShapeDtypeStruct((n_seq, n_st, ch), F32)),
        scratch_shapes=[pltpu.VMEM((sb, HALO + n_new, ch), F32)],
        compiler_params=_cparams(1), name="conv_sample",
    )(state, u, wdw, bdw, lng, lnb)


def _merge_body(n_a, d_model, xa_ref, xb_ref, csa_ref, csb_ref, oa_ref, ob_ref, g_ref,
                wpw_ref, wao_ref, wout_ref, gffn_ref, wr12_ref, wr1_ref, br_ref,
                xm_ref, hf_ref, lg_ref):
    i = pl.program_id(0)
    is_a = i < n_a
    x = jnp.where(is_a, xa_ref[...], xb_ref[...])
    cs = jnp.where(is_a, csa_ref[...], csb_ref[...])
    o = jnp.where(is_a, oa_ref[...], ob_ref[...])
    conv = _dot(cs, wpw_ref[...])
    att = _dot(o, wao_ref[...])
    mix = g_ref[:, 0:d_model].astype(F32) * conv + g_ref[:, d_model:2 * d_model].astype(F32) * att
    xm = x + _dot(mix.astype(BF16), wout_ref[...])
    xm_ref[...] = xm
    hf = _rms(xm, gffn_ref[...])
    hf_ref[...] = hf
    h1 = hf.astype(BF16)
    h2 = (hf - h1.astype(F32)).astype(BF16)
    big = _dot(h1, wr12_ref[...])
    lg_ref[...] = big[:, 0:LANES] + big[:, LANES:2 * LANES] + _dot(h2, wr1_ref[...]) + br_ref[...]


def _merge(xa, xb, csa, csb, oa, ob, g, wpw, wao, wout, gffn, wr12, wr1, br, n_a, tm):
    d_model = xa.shape[1]
    n = g.shape[0]
    n_t = n // tm
    row = lambda i: (i, 0)
    ra = lambda i: (jnp.minimum(i, n_a - 1), 0)
    rb = lambda i: (jnp.maximum(i - n_a, 0), 0)
    full = lambda i: (0, 0)
    dc, da = csa.shape[1], oa.shape[1]
    return pl.pallas_call(
        functools.partial(_merge_body, n_a, d_model),
        grid=(n_t,),
        in_specs=[pl.BlockSpec((tm, d_model), ra), pl.BlockSpec((tm, d_model), rb),
                  pl.BlockSpec((tm, dc), ra), pl.BlockSpec((tm, dc), rb),
                  pl.BlockSpec((tm, da), ra), pl.BlockSpec((tm, da), rb),
                  pl.BlockSpec((tm, 2 * d_model), row),
                  pl.BlockSpec((dc, d_model), full), pl.BlockSpec((da, d_model), full),
                  pl.BlockSpec((d_model, d_model), full), pl.BlockSpec((1, d_model), full),
                  pl.BlockSpec((d_model, 2 * LANES), full), pl.BlockSpec((d_model, LANES), full),
                  pl.BlockSpec((1, LANES), full)],
        out_specs=(pl.BlockSpec((tm, d_model), row), pl.BlockSpec((tm, d_model), row),
                   pl.BlockSpec((tm, LANES), row)),
        out_shape=(jax.ShapeDtypeStruct((n, d_model), F32), jax.ShapeDtypeStruct((n, d_model), F32),
                   jax.ShapeDtypeStruct((n, LANES), F32)),
        compiler_params=_cparams(1), name="merge_router",
    )(xa, xb, csa, csb, oa, ob, g, wpw, wao, wout, gffn, wr12, wr1, br)


ROUTE_IDX, ROUTE_RANK, ROUTE_GATE = 0, TOP_K, 2 * TOP_K


def _route_body(lg_ref, route_ref, cnt_ref, carry_ref):
    i = pl.program_id(0)

    @pl.when(i == 0)
    def _():
        carry_ref[...] = jnp.zeros(carry_ref.shape, F32)

    lg = lg_ref[...]
    tm = lg.shape[0]
    lane = lax.broadcasted_iota(I32, lg.shape, 1).astype(F32)
    vals, sels, idxs = [], [], []
    for _ in range(TOP_K):
        mk = jnp.max(lg, axis=1, keepdims=True)
        ik = jnp.min(jnp.where(lg == mk, lane, float(LANES)), axis=1, keepdims=True)
        sel = lane == ik
        vals.append(mk)
        idxs.append(ik)
        sels.append(sel)
        lg = jnp.where(sel, -3e38, lg)
    onehot = sum(s.astype(F32) for s in sels)
    es = [jnp.exp(v - vals[0]) for v in vals]
    den = sum(es)
    ri = lax.broadcasted_iota(I32, (tm, tm), 0)
    rj = lax.broadcasted_iota(I32, (tm, tm), 1)
    before = (rj < ri).astype(BF16)
    carry = carry_ref[0:1, :]
    cum = _dot(before, onehot.astype(BF16)) + carry
    route = jnp.zeros(lg.shape, F32)
    for k in range(TOP_K):
        rank = jnp.sum(jnp.where(sels[k], cum, 0.0), axis=1, keepdims=True)
        route = jnp.where(lane == ROUTE_IDX + k, idxs[k], route)
        route = jnp.where(lane == ROUTE_RANK + k, rank, route)
        route = jnp.where(lane == ROUTE_GATE + k, es[k] / den, route)
    route_ref[...] = route
    total = carry + jnp.sum(onehot, axis=0, keepdims=True)
    carry_ref[...] = jnp.broadcast_to(total, carry_ref.shape)
    cnt_ref[...] = jnp.broadcast_to(total, cnt_ref.shape)


def _route(logits, tm):
    n = logits.shape[0]
    return pl.pallas_call(
        _route_body,
        grid=(n // tm,),
        in_specs=[pl.BlockSpec((tm, LANES), lambda i: (i, 0))],
        out_specs=(pl.BlockSpec((tm, LANES), lambda i: (i, 0)), pl.BlockSpec((8, LANES), lambda i: (0, 0))),
        out_shape=(jax.ShapeDtypeStruct((n, LANES), F32), jax.ShapeDtypeStruct((8, LANES), F32)),
        scratch_shapes=[pltpu.VMEM((8, LANES), F32)],
        compiler_params=_cparams(1), name="route_topk",
    )(logits)


def _dispatch_body(tm, te, n_blocks, pend_ref, dest_ref, hf_ref, xs_ref, zero_ref, sem):
    i = pl.program_id(0)

    @pl.when(i == 0)
    def _():
        zero_ref[...] = jnp.zeros(zero_ref.shape, F32)

        def zero_block(start):
            return pltpu.make_async_copy(zero_ref, xs_ref.at[pl.ds(pl.multiple_of(start, te), te), :], sem)

        def tails(wait):
            def body(e, c):
                @pl.when(pend_ref[e + 1] > pend_ref[e])
                def _():
                    cp = zero_block(pend_ref[e + 1] - te)
                    cp.wait() if wait else cp.start()
                return c
            lax.fori_loop(0, N_EXPERTS, body, 0)

        def unused(wait):
            def body(bk, c):
                cp = zero_block(bk * te)
                cp.wait() if wait else cp.start()
                return c
            lax.fori_loop(lax.div(pend_ref[N_EXPERTS], te), n_blocks, body, 0)

        tails(False)
        unused(False)
        tails(True)
        unused(True)

    def row_copy(r, k):
        return pltpu.make_async_copy(hf_ref.at[pl.ds(r, 1), :], xs_ref.at[pl.ds(dest_ref[r * TOP_K + k], 1), :], sem)

    def rows(wait):
        def body(r, c):
            for k in range(TOP_K):
                cp = row_copy(r, k)
                cp.wait() if wait else cp.start()
            return c
        lax.fori_loop(0, tm, body, 0, unroll=2)

    rows(False)
    rows(True)


def _dispatch(pend0, dest_flat, hf, n_rows, tm, te):
    n, d_model = hf.shape
    grid_spec = pltpu.PrefetchScalarGridSpec(
        num_scalar_prefetch=1, grid=(n // tm,),
        in_specs=[pl.BlockSpec((tm * TOP_K,), lambda i, pe: (i,), memory_space=pltpu.SMEM),
                  pl.BlockSpec((tm, d_model), lambda i, pe: (i, 0))],
        out_specs=pl.BlockSpec(memory_space=pl.ANY),
        scratch_shapes=[pltpu.VMEM((te, d_model), F32), pltpu.SemaphoreType.DMA(())])
    return pl.pallas_call(
        functools.partial(_dispatch_body, tm, te, n_rows // te),
        grid_spec=grid_spec,
        out_shape=jax.ShapeDtypeStruct((n_rows, d_model), F32),
        compiler_params=_cparams(1), name="moe_dispatch",
    )(pend0, dest_flat, hf)


def _expert_body(d_exp, be_ref, nu_ref, x_ref, wgu_ref, bgu_ref, wdn_ref, bdn_ref, y_ref, wgu_b, wdn_b):
    i = pl.program_id(0)
    used = i < nu_ref[0]
    prev = be_ref[jnp.maximum(i - 1, 0)]
    fresh = jnp.logical_and(used, jnp.logical_or(i == 0, be_ref[i] != prev))

    @pl.when(fresh)
    def _():
        wgu_b[...] = wgu_ref[0].astype(BF16)
        wdn_b[...] = wdn_ref[0].astype(BF16)

    @pl.when(used)
    def _():
        xb = x_ref[...].astype(BF16)
        gu = _dot(xb, wgu_b[...]) + bgu_ref[0]
        gate = jnp.minimum(gu[:, 0:d_exp], SWIGLU_LIMIT)
        lin = jnp.clip(gu[:, d_exp:2 * d_exp], -SWIGLU_LIMIT, SWIGLU_LIMIT)
        act = gate * jax.nn.sigmoid(SWIGLU_ALPHA * gate) * (lin + 1.0)
        y_ref[...] = _dot(act.astype(BF16), wdn_b[...]) + bdn_ref[0]

    @pl.when(jnp.logical_not(used))
    def _():
        y_ref[...] = jnp.zeros(y_ref.shape, F32)


def _experts(block_expert, n_used, xs, wgu, bgu, wdn, bdn, te):
    n_rows, d_model = xs.shape
    d_exp = wdn.shape[1]
    n_blocks = n_rows // te
    rowmap = lambda i, be, nu: (jnp.minimum(i, nu[0] - 1), 0)
    emap = lambda i, be, nu: (be[i], 0, 0)
    grid_spec = pltpu.PrefetchScalarGridSpec(
        num_scalar_prefetch=2, grid=(n_blocks,),
        in_specs=[pl.BlockSpec((te, d_model), rowmap),
                  pl.BlockSpec((1, d_model, 2 * d_exp), emap),
                  pl.BlockSpec((1, 1, 2 * d_exp), emap),
                  pl.BlockSpec((1, d_exp, d_model), emap),
                  pl.BlockSpec((1, 1, d_model), emap)],
        out_specs=pl.BlockSpec((te, d_model), lambda i, be, nu: (i, 0)),
        scratch_shapes=[pltpu.VMEM((d_model, 2 * d_exp), BF16), pltpu.VMEM((d_exp, d_model), BF16)])
    return pl.pallas_call(
        functools.partial(_expert_body, d_exp),
        grid_spec=grid_spec,
        out_shape=jax.ShapeDtypeStruct((n_rows, d_model), F32),
        compiler_params=_cparams(1), name="moe_experts",
    )(block_expert, n_used, xs, wgu, bgu, wdn, bdn)


def _combine_body(n_a, tm, dest_ref, ys_ref, route_ref, xm_ref, gfin_ref, ya_ref, yb_ref, buf_ref, sem):
    i = pl.program_id(0)

    def row_copy(r, k):
        d = dest_ref[r * TOP_K + k]
        return pltpu.make_async_copy(ys_ref.at[pl.ds(d, 1), :], buf_ref.at[k, pl.ds(r, 1), :], sem)

    def issue(r, c):
        for k in range(TOP_K):
            row_copy(r, k).start(priority=k % 2)
        return c

    def drain(r, c):
        for k in range(TOP_K):
            row_copy(r, k).wait()
        return c

    lax.fori_loop(0, tm, issue, 0)
    lax.fori_loop(0, tm, drain, 0)
    route = route_ref[...]
    y = jnp.zeros(xm_ref.shape, F32)
    for k in range(TOP_K):
        y = y + buf_ref[k] * route[:, ROUTE_GATE + k:ROUTE_GATE + k + 1]
    out = _rms(xm_ref[...] + y, gfin_ref[...])

    @pl.when(i < n_a)
    def _():
        ya_ref[...] = out

    @pl.when(i >= n_a)
    def _():
        yb_ref[...] = out


def _combine(dest_flat, ys, route, xm, gfin, n_a, tm):
    n, d_model = xm.shape
    n_t = n // tm
    n_b = n_t - n_a
    return pl.pallas_call(
        functools.partial(_combine_body, n_a, tm),
        grid=(n_t,),
        in_specs=[pl.BlockSpec((tm * TOP_K,), lambda i: (i,), memory_space=pltpu.SMEM),
                  pl.BlockSpec(memory_space=pl.ANY),
                  pl.BlockSpec((tm, LANES), lambda i: (i, 0)),
                  pl.BlockSpec((tm, d_model), lambda i: (i, 0)),
                  pl.BlockSpec((1, d_model), lambda i: (0, 0))],
        out_specs=(pl.BlockSpec((tm, d_model), lambda i: (jnp.minimum(i, n_a - 1), 0)),
                   pl.BlockSpec((tm, d_model), lambda i: (jnp.maximum(i - n_a, 0), 0))),
        out_shape=(jax.ShapeDtypeStruct((n_a * tm, d_model), F32),
                   jax.ShapeDtypeStruct((n_b * tm, d_model), F32)),
        scratch_shapes=[pltpu.VMEM((TOP_K, tm, d_model), F32), pltpu.SemaphoreType.DMA(())],
        compiler_params=_cparams(1), name="moe_combine",
    )(dest_flat, ys, route, xm, gfin)


def _prep_in_weights(w_in, b_forget, d_model):
    d_conv = d_model // 2
    s0, s1, s2 = ATT_WIDTH, 2 * ATT_WIDTH, 3 * ATT_WIDTH
    s3 = s2 + N_HEADS
    s4 = s3 + 2 * d_conv
    wf = w_in[:, s2:s3]
    wf_pad = jnp.pad(wf, ((0, 0), (0, LANES - N_HEADS)))
    w_all = jnp.concatenate([w_in[:, :s2], w_in[:, s3:s4], w_in[:, s4:], wf_pad], axis=1).astype(BF16)
    wft = jnp.pad(wf.T, ((0, HEAD_ROWS - N_HEADS), (0, 0))).astype(BF16)
    bfr = jnp.pad(b_forget, (0, LANES - N_HEADS))[None, :]
    bfc = jnp.broadcast_to(jnp.pad(b_forget, (0, HEAD_ROWS - N_HEADS))[:, None], (HEAD_ROWS, LANES))
    return w_all, wft, bfr, bfc


def _tri_matrices(tm, group):
    i = jnp.arange(tm)
    upper = i[:, None] <= i[None, :]
    same = (i[:, None] // group) == (i[None, :] // group)
    return jnp.stack([upper, jnp.logical_and(upper, same)]).astype(BF16)


def _forward(x_prompt, x_sample, cache_k, cache_v, cache_logf, state_conv, page_table, meta_tokens,
             g_mix, w_in, b_forget, w_dw, b_dw, ln_g, ln_b, w_pw2, w_att_o, w_out, g_ffn,
             w_router, b_router, w_gate_up, b_gate_up, w_down, b_down, g_final,
             tm, tq, pps, t_page, sb, t_route, t_disp, te, t_comb):
    n_batch, seq, d_model = x_prompt.shape
    n_seq, n_new, _ = x_sample.shape
    n_pages = page_table.shape[1]
    n_phys = cache_k.shape[1]
    d_conv = d_model // 2
    n_p = n_batch * seq
    n_s = n_seq * n_new
    n_a = n_p // tm
    layer = 0

    xa = x_prompt.reshape(n_p, d_model)
    xb = x_sample.reshape(n_s, d_model)
    gmix = g_mix[layer][None, :]
    w_all, wft, bfr, bfc = _prep_in_weights(w_in[layer], b_forget[layer], d_model)

    meta = meta_tokens.astype(F32)
    (_, k_m, v_m, kb_m, vb_m, lf_m, lft_m, u_m, _) = _in_project(
        meta, meta, 1, 0, N_META, gmix, w_all, wft, bfr, bfc)
    (q, k, v, kb, vb, lf, lft, u, g) = _in_project(xa, xb, n_a, n_s // tm, tm, gmix, w_all, wft, bfr, bfc)

    tri_m = _tri_matrices(N_META, N_META)
    ct_m = _cumsum_tokens(lft_m, tri_m, jnp.zeros((HEAD_ROWS, LANES), F32), 1, 1, N_META)
    init = jnp.broadcast_to(ct_m[:, N_META - 1:N_META], (HEAD_ROWS, LANES))
    ct = _cumsum_tokens(lft, _tri_matrices(tm, n_new), init, n_a, seq // tm, tm)

    o_p = _prompt_attention(q, kb, vb, ct, kb_m, vb_m, ct_m, n_batch, seq, tq)

    pt_flat = page_table.reshape(-1).astype(I32)
    lf_pages_t = jnp.swapaxes(cache_logf[layer], 1, 2)
    cache_kt = jnp.transpose(cache_k[layer], (0, 2, 3, 1))
    cache_vt = jnp.transpose(cache_v[layer], (0, 2, 3, 1))
    page_cw = _page_cumsum(lf_pages_t, t_page)
    n_rows = N_HEADS * n_new
    cn = jnp.transpose(ct[:N_HEADS, n_p:].reshape(N_HEADS, n_seq, n_new), (1, 0, 2))
    cn_col = cn.reshape(n_seq, n_rows, 1)
    cn_rows = jnp.broadcast_to(cn[:, :, None, :], (n_seq, N_HEADS, n_new, n_new)).reshape(n_seq, n_rows, n_new)
    cn_rows = jnp.pad(cn_rows, ((0, 0), (0, 0), (0, HEAD_ROWS - n_new)))
    k_s = k[n_p:].reshape(n_seq, n_new, ATT_WIDTH)
    v_s = v[n_p:].reshape(n_seq, n_new, ATT_WIDTH)
    q_s = q[n_p:].astype(F32).reshape(n_seq, n_new, ATT_WIDTH)
    o_s = _sample_attention(pt_flat, cache_kt, cache_vt, page_cw, q_s, k_s, v_s, cn_rows, cn_col,
                            n_seq, n_pages, n_new, pps)
    o_s = o_s.reshape(n_s, ATT_WIDTH).astype(BF16)

    wdw = jnp.pad(w_dw[layer], ((0, 1), (0, 0)))
    bdw, lng, lnb = b_dw[layer][None, :], ln_g[layer][None, :], ln_b[layer][None, :]
    state0 = jnp.concatenate([jnp.zeros((HALO - N_META, d_conv), F32), u_m], axis=0)
    cs_p = _conv_prompt(u, state0, wdw, bdw, lng, lnb, n_batch, seq, tm)
    u_s = u[n_p:].reshape(n_seq, n_new, d_conv)
    cs_s, conv_sample = _conv_sample(state_conv[layer], u_s, wdw, bdw, lng, lnb, sb)
    cs_s = cs_s.reshape(n_s, d_conv)

    wr = jnp.pad(w_router[layer], ((0, 0), (0, LANES - N_EXPERTS)))
    wr1 = wr.astype(BF16)
    wr2 = (wr - wr1.astype(F32)).astype(BF16)
    wr12 = jnp.concatenate([wr1, wr2], axis=1)
    br = jnp.pad(b_router[layer], (0, LANES - N_EXPERTS), constant_values=NEG_BIG)[None, :]
    xm, hf, logits = _merge(xa, xb, cs_p, cs_s, o_p, o_s, g,
                            w_pw2[layer].astype(BF16), w_att_o[layer].astype(BF16), w_out[layer].astype(BF16),
                            g_ffn[layer][None, :], wr12, wr1, br, n_a, tm)

    n_all = n_p + n_s
    route, counts = _route(logits, t_route)
    cnt = counts[0, :N_EXPERTS].astype(I32)
    padded = (cnt + te - 1) // te * te
    pend = jnp.cumsum(padded)
    pstart = pend - padded
    idx = route[:, ROUTE_IDX:ROUTE_IDX + TOP_K].astype(I32)
    rank = route[:, ROUTE_RANK:ROUTE_RANK + TOP_K].astype(I32)
    dest = (pstart[idx] + rank).reshape(-1).astype(I32)
    n_blocks = -(-(n_all * TOP_K) // te) + N_EXPERTS
    n_used = (pend[-1] // te).astype(I32).reshape(1)
    blk = jnp.minimum(jnp.arange(n_blocks, dtype=I32), n_used[0] - 1) * te
    block_expert = jnp.minimum(jnp.sum(pend[None, :] <= blk[:, None], axis=1), N_EXPERTS - 1).astype(I32)
    pend0 = jnp.concatenate([jnp.zeros((1,), I32), pend.astype(I32)])
    xs = _dispatch(pend0, dest, hf, n_blocks * te, t_disp, te)
    ys = _experts(block_expert, n_used, xs, w_gate_up[layer], b_gate_up[layer][:, None, :],
                  w_down[layer], b_down[layer][:, None, :], te)
    y_p, y_s = _combine(dest, ys, route, xm, g_final[None, :], n_p // t_comb, t_comb)

    def with_meta(real, m):
        real = real.reshape((n_batch, seq) + real.shape[1:])
        m = jnp.broadcast_to(m[None], (n_batch,) + m.shape)
        return jnp.concatenate([m, real], axis=1)

    hd = (N_HEADS, HEAD_DIM)
    k_prompt = with_meta(k[:n_p], k_m).reshape((1, n_batch, seq + N_META) + hd)
    v_prompt = with_meta(v[:n_p], v_m).reshape((1, n_batch, seq + N_META) + hd)
    logf_prompt = with_meta(lf[:n_p, :N_HEADS], lf_m[:, :N_HEADS])[None]
    n_st = CONV_WIDTH - 1
    conv_prompt = u[:n_p].reshape(n_batch, seq, d_conv)[:, seq - n_st:][None]
    k_sample = k_s.reshape((1, n_seq, n_new) + hd)
    v_sample = v_s.reshape((1, n_seq, n_new) + hd)
    logf_sample = lf[n_p:, :N_HEADS].reshape(1, n_seq, n_new, N_HEADS)
    return (y_p.reshape(n_batch, seq, d_model), y_s.reshape(n_seq, n_new, d_model),
            k_prompt, v_prompt, logf_prompt, conv_prompt, k_sample, v_sample, logf_sample, conv_sample[None])


def kernel(x_prompt, x_sample, cache_k, cache_v, cache_logf, state_conv, page_table, meta_tokens,
           g_mix, w_in, b_forget, w_dw, b_dw, ln_g, ln_b, w_pw2, w_att_o, w_out, g_ffn,
           w_router, b_router, w_gate_up, b_gate_up, w_down, b_down, g_final):
    return _forward(x_prompt, x_sample, cache_k, cache_v, cache_logf, state_conv, page_table, meta_tokens,
                    g_mix, w_in, b_forget, w_dw, b_dw, ln_g, ln_b, w_pw2, w_att_o, w_out, g_ffn,
                    w_router, b_router, w_gate_up, b_gate_up, w_down, b_down, g_final,
                    tm=512, tq=512, pps=16, t_page=256, sb=16, t_route=512, t_disp=256, te=256, t_comb=128)
```

```python
import functools

import jax
import jax.numpy as jnp
from jax import lax
from jax.experimental import pallas as pl
from jax.experimental.pallas import tpu as pltpu

F32 = jnp.float32
BF16 = jnp.bfloat16
I32 = jnp.int32

N_META = 16
N_HEADS = 8
HEAD_DIM = 64
ATT_WIDTH = N_HEADS * HEAD_DIM
CONV_WIDTH = 31
N_EXPERTS = 32
TOP_K = 4
SWIGLU_LIMIT = 7.0
SWIGLU_ALPHA = 1.702
NORM_EPS = 1e-5
SCALE = HEAD_DIM ** -0.5
PAGE_SIZE = 128

LANES = 128
SUBLANES = 8
HEAD_ROWS = 16
NEG_BIG = -1e30
VMEM_LIMIT = 56 * 1024 * 1024


def _cparams(n_axes):
    return pltpu.CompilerParams(dimension_semantics=("arbitrary",) * n_axes,
                                vmem_limit_bytes=VMEM_LIMIT)


def _dot(a, b):
    return jnp.dot(a, b, preferred_element_type=F32)


def _dot_nt(a, b):
    return lax.dot_general(a, b, (((1,), (1,)), ((), ())), preferred_element_type=F32)


def _split3(x):
    a = x.astype(BF16)
    r = x - a.astype(F32)
    b = r.astype(BF16)
    c = (r - b.astype(F32)).astype(BF16)
    return a, b, c


def _dot_exact_rhs01(x, m):
    a, b, c = _split3(x)
    return _dot(a, m) + _dot(b, m) + _dot(c, m)


def _dot_exact_lhs01(m, x):
    a, b, c = _split3(x)
    return _dot(m, a) + _dot(m, b) + _dot(m, c)


def _log_sigmoid(x):
    return jnp.minimum(x, 0.0) - jnp.log1p(jnp.exp(-jnp.abs(x)))


def _rms(x, g):
    ms = jnp.mean(x * x, axis=-1, keepdims=True)
    return x * lax.rsqrt(ms + NORM_EPS) * g


def _inproj_body(n_a, n_b, d_att, d_conv, d_model,
                 xa_ref, xb_ref, gmix_ref, w_ref, wft_ref, bfc_ref, *out_refs):
    if n_b:
        q_ref, ka_ref, kb2_ref, va_ref, vb2_ref, kb_ref, vb_ref, lft_ref, ua_ref, ub2_ref, g_ref = out_refs
    else:
        q_ref, ka_ref, va_ref, kb_ref, vb_ref, lft_ref, ua_ref, g_ref = out_refs
        kb2_ref = vb2_ref = ub2_ref = None
    i = pl.program_id(0)
    is_a = i < n_a

    def put(ref_a, ref_b, val):
        if ref_b is None:
            ref_a[...] = val
            return

        @pl.when(is_a)
        def _():
            ref_a[...] = val

        @pl.when(jnp.logical_not(is_a))
        def _():
            ref_b[...] = val

    x = jnp.where(is_a, xa_ref[...], xb_ref[...])
    hb = _rms(x, gmix_ref[...]).astype(BF16)
    o = 0
    q_ref[...] = (_dot(hb, w_ref[:, o:o + d_att]) * SCALE).astype(BF16)
    o += d_att
    kk = _dot(hb, w_ref[:, o:o + d_att])
    put(ka_ref, kb2_ref, kk)
    kb_ref[...] = kk.astype(BF16)
    o += d_att
    vv = _dot(hb, w_ref[:, o:o + d_att])
    put(va_ref, vb2_ref, vv)
    vb_ref[...] = vv.astype(BF16)
    o += d_att
    ua = _dot(hb, w_ref[:, o:o + d_conv])
    ub = _dot(hb, w_ref[:, o + d_conv:o + 2 * d_conv])
    put(ua_ref, ub2_ref, ua * jax.nn.sigmoid(ub))
    o += 2 * d_conv
    for c in range(0, 2 * d_model, 512):
        g_ref[:, c:c + 512] = jax.nn.sigmoid(_dot(hb, w_ref[:, o + c:o + c + 512])).astype(BF16)
    lft_ref[...] = _log_sigmoid(_dot_nt(wft_ref[...], hb) + bfc_ref[:, 0:1])


def _in_project(xa, xb, n_a, n_b, tm, gmix, w_all, wft, bfc):
    d_model = xa.shape[1]
    n_t = n_a + n_b
    n = n_t * tm
    d_att, d_conv = ATT_WIDTH, d_model // 2
    wcols = w_all.shape[1]
    row = lambda i: (i, 0)
    ra = lambda i: (jnp.minimum(i, n_a - 1), 0)
    rb = lambda i: (jnp.maximum(i - n_a, 0), 0)
    full = lambda i: (0, 0)

    def split(width):
        out = [(jax.ShapeDtypeStruct((n_a * tm, width), F32), pl.BlockSpec((tm, width), ra))]
        if n_b:
            out.append((jax.ShapeDtypeStruct((n_b * tm, width), F32), pl.BlockSpec((tm, width), rb)))
        return out

    def whole(shape, block, index_map, dtype):
        return [(jax.ShapeDtypeStruct(shape, dtype), pl.BlockSpec(block, index_map))]

    outs = (whole((n, d_att), (tm, d_att), row, BF16)
            + split(d_att) + split(d_att)
            + whole((n, d_att), (tm, d_att), row, BF16)
            + whole((n, d_att), (tm, d_att), row, BF16)
            + whole((HEAD_ROWS, n), (HEAD_ROWS, tm), lambda i: (0, i), F32)
            + split(d_conv)
            + whole((n, 2 * d_model), (tm, 2 * d_model), row, BF16))
    in_specs = [
        pl.BlockSpec((tm, d_model), ra),
        pl.BlockSpec((tm, d_model), rb),
        pl.BlockSpec((1, d_model), full),
        pl.BlockSpec((d_model, wcols), full),
        pl.BlockSpec((HEAD_ROWS, d_model), full),
        pl.BlockSpec((HEAD_ROWS, LANES), full),
    ]
    return pl.pallas_call(
        functools.partial(_inproj_body, n_a, n_b, d_att, d_conv, d_model),
        grid=(n_t,), in_specs=in_specs, out_specs=tuple(s for _, s in outs),
        out_shape=tuple(s for s, _ in outs),
        compiler_params=_cparams(1), name="in_project",
    )(xa, xb, gmix, w_all, wft, bfc)


def _cumsum_body(n_a, seg, lft_ref, tri_ref, init_ref, ct_ref, carry_ref):
    i = pl.program_id(0)
    is_a = i < n_a
    start = jnp.logical_and(is_a, i % seg == 0)
    carry = jnp.where(start, init_ref[...], carry_ref[...])
    carry = jnp.where(is_a, carry, 0.0)
    c = _dot_exact_rhs01(lft_ref[...], tri_ref[0]) + carry[:, 0:1]
    ct_ref[...] = c
    tm = c.shape[1]
    carry_ref[...] = jnp.broadcast_to(c[:, tm - 1:tm], carry_ref.shape)


def _cumsum_tokens(lft, tris, init, n_a, seg, tm):
    n = lft.shape[1]
    return pl.pallas_call(
        functools.partial(_cumsum_body, n_a, seg),
        grid=(n // tm,),
        in_specs=[pl.BlockSpec((HEAD_ROWS, tm), lambda i: (0, i)),
                  pl.BlockSpec((1, tm, tm), lambda i: (jnp.where(i < n_a, 0, 1), 0, 0)),
                  pl.BlockSpec((HEAD_ROWS, LANES), lambda i: (0, 0))],
        out_specs=pl.BlockSpec((HEAD_ROWS, tm), lambda i: (0, i)),
        out_shape=jax.ShapeDtypeStruct((HEAD_ROWS, n), F32),
        scratch_shapes=[pltpu.VMEM((HEAD_ROWS, LANES), F32)],
        compiler_params=_cparams(1), name="decay_cumsum",
    )(lft, tris, init)


def _attn_body(tq, q_ref, k_ref, v_ref, ct_ref, km_ref, vm_ref, ctm_ref, o_ref):
    hp = pl.program_id(1)
    i = pl.program_id(2)
    lane = lax.broadcasted_iota(I32, (1, LANES), 1)
    q2 = q_ref[...]
    km = km_ref[...]
    vm = vm_ref[...]
    rows = lax.broadcasted_iota(I32, (tq, tq), 0)
    cols = lax.broadcasted_iota(I32, (tq, tq), 1)
    outs = []
    for hh in range(2):
        in_head = (lane < HEAD_DIM) if hh == 0 else (lane >= HEAD_DIM)
        qh = jnp.where(in_head, q2, jnp.zeros_like(q2))
        head = 2 * hp + hh
        s = _dot_nt(qh, km) - ctm_ref[pl.ds(head, 1), :]
        m = jnp.max(s, axis=1, keepdims=True)
        p = jnp.exp(s - m)
        l = jnp.sum(p, axis=1, keepdims=True)
        acc = _dot(p.astype(BF16), vm)

        def tile(j, carry, diag):
            m, l, acc = carry
            k0 = pl.multiple_of(j * tq, tq)
            s = _dot_nt(qh, k_ref[pl.ds(k0, tq), :]) - ct_ref[pl.ds(head, 1), pl.ds(k0, tq)]
            if diag:
                s = jnp.where(cols <= rows, s, NEG_BIG)
            mn = jnp.maximum(m, jnp.max(s, axis=1, keepdims=True))
            a = jnp.exp(m - mn)
            p = jnp.exp(s - mn)
            l = a * l + jnp.sum(p, axis=1, keepdims=True)
            acc = a * acc + _dot(p.astype(BF16), v_ref[pl.ds(k0, tq), :])
            return mn, l, acc

        carry = lax.fori_loop(0, i, lambda j, c: tile(j, c, False), (m, l, acc))
        m, l, acc = tile(i, carry, True)
        outs.append(acc / l)
    o_ref[...] = jnp.where(lane < HEAD_DIM, outs[0], outs[1]).astype(o_ref.dtype)


def _prompt_attention(q, kb, vb, ct, kb_m, vb_m, ct_m, n_batch, seq, tq):
    n_hp = ATT_WIDTH // LANES
    n_q = seq // tq
    return pl.pallas_call(
        functools.partial(_attn_body, tq),
        grid=(n_batch, n_hp, n_q),
        in_specs=[pl.BlockSpec((tq, LANES), lambda b, h, i: (b * n_q + i, h)),
                  pl.BlockSpec((seq, LANES), lambda b, h, i: (b, h)),
                  pl.BlockSpec((seq, LANES), lambda b, h, i: (b, h)),
                  pl.BlockSpec((HEAD_ROWS, seq), lambda b, h, i: (0, b)),
                  pl.BlockSpec((N_META, LANES), lambda b, h, i: (0, h)),
                  pl.BlockSpec((N_META, LANES), lambda b, h, i: (0, h)),
                  pl.BlockSpec((HEAD_ROWS, N_META), lambda b, h, i: (0, 0))],
        out_specs=pl.BlockSpec((tq, LANES), lambda b, h, i: (b * n_q + i, h)),
        out_shape=jax.ShapeDtypeStruct((n_batch * seq, ATT_WIDTH), BF16),
        compiler_params=_cparams(3), name="prompt_attention",
    )(q, kb, vb, ct, kb_m, vb_m, ct_m)


def _page_cumsum_body(x_ref, o_ref):
    li = lax.broadcasted_iota(I32, (LANES, LANES), 0)
    lj = lax.broadcasted_iota(I32, (LANES, LANES), 1)
    o_ref[...] = _dot_exact_rhs01(x_ref[...], (li <= lj).astype(BF16))


def _page_cumsum(lf_pages_t, tp):
    n_phys = lf_pages_t.shape[0]
    x = lf_pages_t.reshape(n_phys * N_HEADS, PAGE_SIZE)
    rows = tp * N_HEADS
    out = pl.pallas_call(
        _page_cumsum_body,
        grid=(n_phys // tp,),
        in_specs=[pl.BlockSpec((rows, PAGE_SIZE), lambda i: (i, 0))],
        out_specs=pl.BlockSpec((rows, PAGE_SIZE), lambda i: (i, 0)),
        out_shape=jax.ShapeDtypeStruct(x.shape, F32),
        compiler_params=_cparams(1), name="page_decay_cumsum",
    )(x)
    return out.reshape(n_phys, N_HEADS, PAGE_SIZE)


def _sample_attn_body(pps, n_new, pt_ref, q_ref, kn_ref, vn_ref, cnr_ref, cnc_ref,
                      ck_ref, cv_ref, cw_ref, o_ref, m_ref, l_ref, acc_ref, run_ref, kbuf, vbuf, cbuf, sem):
    g = pl.program_id(1)
    n_g = pl.num_programs(1)
    step = pl.program_id(0) * n_g + g
    n_steps = pl.num_programs(0) * n_g
    slot = step % 2

    def page_copies(s, sl):
        out = []
        for p in range(pps):
            page = pt_ref[s * pps + p]
            out.append(pltpu.make_async_copy(ck_ref.at[page], kbuf.at[sl, p], sem.at[sl, 0]))
            out.append(pltpu.make_async_copy(cv_ref.at[page], vbuf.at[sl, p], sem.at[sl, 1]))
            out.append(pltpu.make_async_copy(cw_ref.at[page], cbuf.at[sl, p], sem.at[sl, 2]))
        return out

    @pl.when(step == 0)
    def _():
        for c in page_copies(step, slot):
            c.start()

    @pl.when(step + 1 < n_steps)
    def _():
        for c in page_copies(step + 1, 1 - slot):
            c.start()

    for c in page_copies(step, slot):
        c.wait()

    n_rows = N_HEADS * n_new
    row = lax.broadcasted_iota(I32, (n_rows, ATT_WIDTH), 0)
    col = lax.broadcasted_iota(I32, (n_rows, ATT_WIDTH), 1)
    own_head = (row // n_new) == (col // HEAD_DIM)
    qbd = jnp.where(own_head, jnp.concatenate([q_ref[0]] * N_HEADS, axis=0), 0.0).astype(BF16)

    @pl.when(g == 0)
    def _():
        m_ref[...] = jnp.full(m_ref.shape, NEG_BIG, F32)
        l_ref[...] = jnp.zeros(l_ref.shape, F32)
        acc_ref[...] = jnp.zeros(acc_ref.shape, F32)
        run_ref[...] = jnp.zeros(run_ref.shape, F32)

    def per_row(x):
        return jnp.broadcast_to(x[:, None, :], (N_HEADS, n_new, PAGE_SIZE)).reshape(n_rows, PAGE_SIZE)

    kt = jnp.concatenate([kbuf[slot, p].reshape(ATT_WIDTH, PAGE_SIZE).astype(BF16) for p in range(pps)], axis=1)
    vt = jnp.concatenate([vbuf[slot, p].reshape(ATT_WIDTH, PAGE_SIZE).astype(BF16) for p in range(pps)], axis=1)
    run = run_ref[...]
    biases = []
    for p in range(pps):
        cw = cbuf[slot, p]
        biases.append(per_row(cw + run))
        run = run + jnp.broadcast_to(cw[:, PAGE_SIZE - 1:PAGE_SIZE], run.shape)
    run_ref[...] = run
    bias = jnp.concatenate(biases, axis=1)
    s = _dot(qbd, kt) - bias
    m_old = m_ref[:, 0:1]
    mn = jnp.maximum(m_old, jnp.max(s, axis=1, keepdims=True))
    a = jnp.exp(m_old - mn)
    p = jnp.exp(s - mn)
    l_new = a * l_ref[:, 0:1] + jnp.sum(p, axis=1, keepdims=True)
    acc_new = a * acc_ref[...] + _dot_nt(p.astype(BF16), vt)
    m_ref[...] = jnp.broadcast_to(mn, m_ref.shape)
    l_ref[...] = jnp.broadcast_to(l_new, l_ref.shape)
    acc_ref[...] = acc_new

    @pl.when(g == n_g - 1)
    def _():
        cn_col = cnc_ref[0][:, 0:1]
        m_past = mn + per_row(run)[:, 0:1] + cn_col
        zpad = jnp.zeros((HEAD_ROWS - n_new, ATT_WIDTH), F32)
        kn = jnp.concatenate([kn_ref[0], zpad], axis=0).astype(BF16)
        vn = jnp.concatenate([vn_ref[0], zpad], axis=0).astype(BF16)
        sn = _dot_nt(qbd, kn) + cn_col - cnr_ref[0]
        r2 = lax.broadcasted_iota(I32, sn.shape, 0) % n_new
        c2 = lax.broadcasted_iota(I32, sn.shape, 1)
        sn = jnp.where(c2 <= r2, sn, NEG_BIG)
        m2 = jnp.maximum(m_past, jnp.max(sn, axis=1, keepdims=True))
        a2 = jnp.exp(m_past - m2)
        pn = jnp.exp(sn - m2)
        l2 = a2 * l_new + jnp.sum(pn, axis=1, keepdims=True)
        acc2 = a2 * acc_new + _dot(pn.astype(BF16), vn)
        o = jnp.where(own_head, acc2 / l2, 0.0)
        o_ref[0] = jnp.sum(o.reshape(N_HEADS, n_new, ATT_WIDTH), axis=0)


def _sample_attention(page_table_flat, cache_kt, cache_vt, page_cw, q_s, k_s, v_s, cn_rows, cn_col,
                      n_seq, n_pages, n_new, pps):
    n_g = n_pages // pps
    n_rows = N_HEADS * n_new

    seq3 = lambda b, g, pt: (b, 0, 0)
    grid_spec = pltpu.PrefetchScalarGridSpec(
        num_scalar_prefetch=1, grid=(n_seq, n_g),
        in_specs=[pl.BlockSpec((1, n_new, ATT_WIDTH), seq3),
                  pl.BlockSpec((1, n_new, ATT_WIDTH), seq3),
                  pl.BlockSpec((1, n_new, ATT_WIDTH), seq3),
                  pl.BlockSpec((1, n_rows, HEAD_ROWS), seq3),
                  pl.BlockSpec((1, n_rows, 1), seq3),
                  pl.BlockSpec(memory_space=pl.ANY), pl.BlockSpec(memory_space=pl.ANY),
                  pl.BlockSpec(memory_space=pl.ANY)],
        out_specs=pl.BlockSpec((1, n_new, ATT_WIDTH), seq3),
        scratch_shapes=[pltpu.VMEM((n_rows, LANES), F32), pltpu.VMEM((n_rows, LANES), F32),
                        pltpu.VMEM((n_rows, ATT_WIDTH), F32),
                        pltpu.VMEM((N_HEADS, PAGE_SIZE), F32),
                        pltpu.VMEM((2, pps, N_HEADS, HEAD_DIM, PAGE_SIZE), F32),
                        pltpu.VMEM((2, pps, N_HEADS, HEAD_DIM, PAGE_SIZE), F32),
                        pltpu.VMEM((2, pps, N_HEADS, PAGE_SIZE), F32),
                        pltpu.SemaphoreType.DMA((2, 3))])
    return pl.pallas_call(
        functools.partial(_sample_attn_body, pps, n_new),
        grid_spec=grid_spec,
        out_shape=jax.ShapeDtypeStruct((n_seq, n_new, ATT_WIDTH), F32),
        compiler_params=_cparams(2), name="sample_attention",
    )(page_table_flat, q_s, k_s, v_s, cn_rows, cn_col, cache_kt, cache_vt, page_cw)


HALO = 32


def _conv_tail(y, bdw, lng, lnb):
    y = y + bdw
    mu = jnp.mean(y, axis=-1, keepdims=True)
    d = y - mu
    var = jnp.mean(d * d, axis=-1, keepdims=True)
    yn = d * lax.rsqrt(var + NORM_EPS) * lng + lnb
    return yn * jax.nn.sigmoid(yn)


def _conv_prompt_body(tm, u_ref, halo_ref, st_ref, wdw_ref, bdw_ref, lng_ref, lnb_ref, cs_ref, ctx_ref, sh_ref):
    i = pl.program_id(1)
    ctx_ref[0:HALO, :] = jnp.where(i == 0, st_ref[...], halo_ref[...])
    ctx_ref[HALO:HALO + tm, :] = u_ref[...]
    n_sh = HALO + tm - SUBLANES
    for b in range(1, SUBLANES):
        sh_ref[b - 1, 0:n_sh, :] = ctx_ref[b:b + n_sh, :]
    off = HALO - (CONV_WIDTH - 1)
    y = jnp.zeros((tm, u_ref.shape[1]), F32)
    for k in range(CONV_WIDTH):
        a, b = divmod(off + k, SUBLANES)
        rows = slice(a * SUBLANES, a * SUBLANES + tm)
        tap = ctx_ref[rows, :] if b == 0 else sh_ref[b - 1, rows, :]
        y = y + tap * wdw_ref[k:k + 1, :]
    cs_ref[...] = _conv_tail(y, bdw_ref[...], lng_ref[...], lnb_ref[...]).astype(cs_ref.dtype)


def _conv_prompt(u, state0, wdw, bdw, lng, lnb, n_batch, seq, tm):
    ch = u.shape[1]
    n_i = seq // tm
    hb = tm // HALO
    full = lambda b, i: (0, 0)
    return pl.pallas_call(
        functools.partial(_conv_prompt_body, tm),
        grid=(n_batch, n_i),
        in_specs=[pl.BlockSpec((tm, ch), lambda b, i: (b * n_i + i, 0)),
                  pl.BlockSpec((HALO, ch), lambda b, i: (jnp.maximum((b * n_i + i) * hb - 1, 0), 0)),
                  pl.BlockSpec((HALO, ch), full),
                  pl.BlockSpec((CONV_WIDTH + 1, ch), full),
                  pl.BlockSpec((1, ch), full), pl.BlockSpec((1, ch), full), pl.BlockSpec((1, ch), full)],
        out_specs=pl.BlockSpec((tm, ch), lambda b, i: (b * n_i + i, 0)),
        out_shape=jax.ShapeDtypeStruct((n_batch * seq, ch), BF16),
        scratch_shapes=[pltpu.VMEM((HALO + tm, ch), F32), pltpu.VMEM((SUBLANES - 1, HALO + tm, ch), F32)],
        compiler_params=_cparams(2), name="conv_prompt",
    )(u, u, state0, wdw, bdw, lng, lnb)


def _conv_sample_body(n_new, st_ref, u_ref, wdw_ref, bdw_ref, lng_ref, lnb_ref, cs_ref, st_out_ref, ctx_ref):
    n_st = CONV_WIDTH - 1
    off = HALO - n_st
    ctx_ref[:, off:HALO, :] = st_ref[...]
    ctx_ref[:, HALO:HALO + n_new, :] = u_ref[...]
    sb, _, ch = u_ref.shape
    y = jnp.zeros((sb, n_new, ch), F32)
    for k in range(CONV_WIDTH):
        y = y + ctx_ref[:, off + k:off + k + n_new, :] * wdw_ref[k:k + 1, :]
    cs_ref[...] = _conv_tail(y, bdw_ref[...], lng_ref[...], lnb_ref[...]).astype(cs_ref.dtype)
    st_out_ref[...] = ctx_ref[:, off + n_new:off + n_new + n_st, :]


def _conv_sample(state, u, wdw, bdw, lng, lnb, sb):
    n_seq, n_new, ch = u.shape
    n_st = CONV_WIDTH - 1
    full = lambda i: (0, 0)
    blk = lambda i: (i, 0, 0)
    return pl.pallas_call(
        functools.partial(_conv_sample_body, n_new),
        grid=(n_seq // sb,),
        in_specs=[pl.BlockSpec((sb, n_st, ch), blk), pl.BlockSpec((sb, n_new, ch), blk),
                  pl.BlockSpec((CONV_WIDTH + 1, ch), full),
                  pl.BlockSpec((1, ch), full), pl.BlockSpec((1, ch), full), pl.BlockSpec((1, ch), full)],
        out_specs=(pl.BlockSpec((sb, n_new, ch), blk), pl.BlockSpec((sb, n_st, ch), blk)),
        out_shape=(jax.ShapeDtypeStruct((n_seq, n_new, ch), BF16),
                   jax.ShapeDtypeStruct((n_seq, n_st, ch), F32)),
        scratch_shapes=[pltpu.VMEM((sb, HALO + n_new, ch), F32)],
        compiler_params=_cparams(1), name="conv_sample",
    )(state, u, wdw, bdw, lng, lnb)


def _merge_body(n_a, d_model, xa_ref, xb_ref, csa_ref, csb_ref, oa_ref, ob_ref, g_ref,
                wpw_ref, wao_ref, wout_ref, gffn_ref, wr12_ref, wr1_ref, br_ref,
                xm_ref, hf_ref, lg_ref):
    i = pl.program_id(0)
    is_a = i < n_a
    x = jnp.where(is_a, xa_ref[...], xb_ref[...])
    cs = jnp.where(is_a, csa_ref[...], csb_ref[...])
    o = jnp.where(is_a, oa_ref[...], ob_ref[...])
    conv = _dot(cs, wpw_ref[...])
    att = _dot(o, wao_ref[...])
    mix = g_ref[:, 0:d_model].astype(F32) * conv + g_ref[:, d_model:2 * d_model].astype(F32) * att
    xm = x + _dot(mix.astype(BF16), wout_ref[...])
    xm_ref[...] = xm
    hf = _rms(xm, gffn_ref[...])
    hf_ref[...] = hf
    h1 = hf.astype(BF16)
    h2 = (hf - h1.astype(F32)).astype(BF16)
    big = _dot(h1, wr12_ref[...])
    lg_ref[...] = big[:, 0:LANES] + big[:, LANES:2 * LANES] + _dot(h2, wr1_ref[...]) + br_ref[...]


def _merge(xa, xb, csa, csb, oa, ob, g, wpw, wao, wout, gffn, wr12, wr1, br, n_a, tm):
    d_model = xa.shape[1]
    n = g.shape[0]
    n_t = n // tm
    row = lambda i: (i, 0)
    ra = lambda i: (jnp.minimum(i, n_a - 1), 0)
    rb = lambda i: (jnp.maximum(i - n_a, 0), 0)
    full = lambda i: (0, 0)
    dc, da = csa.shape[1], oa.shape[1]
    return pl.pallas_call(
        functools.partial(_merge_body, n_a, d_model),
        grid=(n_t,),
        in_specs=[pl.BlockSpec((tm, d_model), ra), pl.BlockSpec((tm, d_model), rb),
                  pl.BlockSpec((tm, dc), ra), pl.BlockSpec((tm, dc), rb),
                  pl.BlockSpec((tm, da), ra), pl.BlockSpec((tm, da), rb),
                  pl.BlockSpec((tm, 2 * d_model), row),
                  pl.BlockSpec((dc, d_model), full), pl.BlockSpec((da, d_model), full),
                  pl.BlockSpec((d_model, d_model), full), pl.BlockSpec((1, d_model), full),
                  pl.BlockSpec((d_model, 2 * LANES), full), pl.BlockSpec((d_model, LANES), full),
                  pl.BlockSpec((1, LANES), full)],
        out_specs=(pl.BlockSpec((tm, d_model), row), pl.BlockSpec((tm, d_model), row),
                   pl.BlockSpec((tm, LANES), row)),
        out_shape=(jax.ShapeDtypeStruct((n, d_model), F32), jax.ShapeDtypeStruct((n, d_model), F32),
                   jax.ShapeDtypeStruct((n, LANES), F32)),
        compiler_params=_cparams(1), name="merge_router",
    )(xa, xb, csa, csb, oa, ob, g, wpw, wao, wout, gffn, wr12, wr1, br)


ROUTE_IDX, ROUTE_RANK, ROUTE_GATE = 0, TOP_K, 2 * TOP_K


def _route_body(lg_ref, route_ref, cnt_ref, carry_ref):
    i = pl.program_id(0)

    @pl.when(i == 0)
    def _():
        carry_ref[...] = jnp.zeros(carry_ref.shape, F32)

    lg = lg_ref[...]
    tm = lg.shape[0]
    lane = lax.broadcasted_iota(I32, lg.shape, 1).astype(F32)
    vals, sels, idxs = [], [], []
    for _ in range(TOP_K):
        mk = jnp.max(lg, axis=1, keepdims=True)
        ik = jnp.min(jnp.where(lg == mk, lane, float(LANES)), axis=1, keepdims=True)
        sel = lane == ik
        vals.append(mk)
        idxs.append(ik)
        sels.append(sel)
        lg = jnp.where(sel, -3e38, lg)
    onehot = sum(s.astype(F32) for s in sels)
    es = [jnp.exp(v - vals[0]) for v in vals]
    den = sum(es)
    ri = lax.broadcasted_iota(I32, (tm, tm), 0)
    rj = lax.broadcasted_iota(I32, (tm, tm), 1)
    before = (rj < ri).astype(BF16)
    carry = carry_ref[0:1, :]
    cum = _dot(before, onehot.astype(BF16)) + carry
    route = jnp.zeros(lg.shape, F32)
    for k in range(TOP_K):
        rank = jnp.sum(jnp.where(sels[k], cum, 0.0), axis=1, keepdims=True)
        route = jnp.where(lane == ROUTE_IDX + k, idxs[k], route)
        route = jnp.where(lane == ROUTE_RANK + k, rank, route)
        route = jnp.where(lane == ROUTE_GATE + k, es[k] / den, route)
    route_ref[...] = route
    total = carry + jnp.sum(onehot, axis=0, keepdims=True)
    carry_ref[...] = jnp.broadcast_to(total, carry_ref.shape)
    cnt_ref[...] = jnp.broadcast_to(total, cnt_ref.shape)


def _route(logits, tm):
    n = logits.shape[0]
    return pl.pallas_call(
        _route_body,
        grid=(n // tm,),
        in_specs=[pl.BlockSpec((tm, LANES), lambda i: (i, 0))],
        out_specs=(pl.BlockSpec((tm, LANES), lambda i: (i, 0)), pl.BlockSpec((8, LANES), lambda i: (0, 0))),
        out_shape=(jax.ShapeDtypeStruct((n, LANES), F32), jax.ShapeDtypeStruct((8, LANES), F32)),
        scratch_shapes=[pltpu.VMEM((8, LANES), F32)],
        compiler_params=_cparams(1), name="route_topk",
    )(logits)


def _dispatch_body(tm, te, n_blocks, pend_ref, dest_ref, hf_ref, xs_ref, zero_ref, sem):
    i = pl.program_id(0)

    @pl.when(i == 0)
    def _():
        zero_ref[...] = jnp.zeros(zero_ref.shape, F32)

        def zero_block(start):
            return pltpu.make_async_copy(zero_ref, xs_ref.at[pl.ds(pl.multiple_of(start, te), te), :], sem)

        def tails(wait):
            def body(e, c):
                @pl.when(pend_ref[e + 1] > pend_ref[e])
                def _():
                    cp = zero_block(pend_ref[e + 1] - te)
                    cp.wait() if wait else cp.start()
                return c
            lax.fori_loop(0, N_EXPERTS, body, 0)

        def unused(wait):
            def body(bk, c):
                cp = zero_block(bk * te)
                cp.wait() if wait else cp.start()
                return c
            lax.fori_loop(lax.div(pend_ref[N_EXPERTS], te), n_blocks, body, 0)

        tails(False)
        unused(False)
        tails(True)
        unused(True)

    def row_copy(r, k):
        return pltpu.make_async_copy(hf_ref.at[pl.ds(r, 1), :], xs_ref.at[pl.ds(dest_ref[r * TOP_K + k], 1), :], sem)

    def rows(wait):
        def body(r, c):
            for k in range(TOP_K):
                cp = row_copy(r, k)
                cp.wait() if wait else cp.start()
            return c
        lax.fori_loop(0, tm, body, 0, unroll=2)

    rows(False)
    rows(True)


def _dispatch(pend0, dest_flat, hf, n_rows, tm, te):
    n, d_model = hf.shape
    grid_spec = pltpu.PrefetchScalarGridSpec(
        num_scalar_prefetch=1, grid=(n // tm,),
        in_specs=[pl.BlockSpec((tm * TOP_K,), lambda i, pe: (i,), memory_space=pltpu.SMEM),
                  pl.BlockSpec((tm, d_model), lambda i, pe: (i, 0))],
        out_specs=pl.BlockSpec(memory_space=pl.ANY),
        scratch_shapes=[pltpu.VMEM((te, d_model), F32), pltpu.SemaphoreType.DMA(())])
    return pl.pallas_call(
        functools.partial(_dispatch_body, tm, te, n_rows // te),
        grid_spec=grid_spec,
        out_shape=jax.ShapeDtypeStruct((n_rows, d_model), F32),
        compiler_params=_cparams(1), name="moe_dispatch",
    )(pend0, dest_flat, hf)


def _expert_body(d_exp, be_ref, nu_ref, x_ref, wgu_ref, bgu_ref, wdn_ref, bdn_ref, y_ref, wgu_b, wdn_b):
    i = pl.program_id(0)
    used = i < nu_ref[0]
    prev = be_ref[jnp.maximum(i - 1, 0)]
    fresh = jnp.logical_and(used, jnp.logical_or(i == 0, be_ref[i] != prev))

    @pl.when(fresh)
    def _():
        wgu_b[...] = wgu_ref[0].astype(BF16)
        wdn_b[...] = wdn_ref[0].astype(BF16)

    @pl.when(used)
    def _():
        xb = x_ref[...].astype(BF16)
        gu = _dot(xb, wgu_b[...]) + bgu_ref[0]
        gate = jnp.minimum(gu[:, 0:d_exp], SWIGLU_LIMIT)
        lin = jnp.clip(gu[:, d_exp:2 * d_exp], -SWIGLU_LIMIT, SWIGLU_LIMIT)
        act = gate * jax.nn.sigmoid(SWIGLU_ALPHA * gate) * (lin + 1.0)
        y_ref[...] = _dot(act.astype(BF16), wdn_b[...]) + bdn_ref[0]

    @pl.when(jnp.logical_not(used))
    def _():
        y_ref[...] = jnp.zeros(y_ref.shape, F32)


def _experts(block_expert, n_used, xs, wgu, bgu, wdn, bdn, te):
    n_rows, d_model = xs.shape
    d_exp = wdn.shape[1]
    n_blocks = n_rows // te
    rowmap = lambda i, be, nu: (jnp.minimum(i, nu[0] - 1), 0)
    emap = lambda i, be, nu: (be[i], 0, 0)
    grid_spec = pltpu.PrefetchScalarGridSpec(
        num_scalar_prefetch=2, grid=(n_blocks,),
        in_specs=[pl.BlockSpec((te, d_model), rowmap),
                  pl.BlockSpec((1, d_model, 2 * d_exp), emap),
                  pl.BlockSpec((1, 1, 2 * d_exp), emap),
                  pl.BlockSpec((1, d_exp, d_model), emap),
                  pl.BlockSpec((1, 1, d_model), emap)],
        out_specs=pl.BlockSpec((te, d_model), lambda i, be, nu: (i, 0)),
        scratch_shapes=[pltpu.VMEM((d_model, 2 * d_exp), BF16), pltpu.VMEM((d_exp, d_model), BF16)])
    return pl.pallas_call(
        functools.partial(_expert_body, d_exp),
        grid_spec=grid_spec,
        out_shape=jax.ShapeDtypeStruct((n_rows, d_model), F32),
        compiler_params=_cparams(1), name="moe_experts",
    )(block_expert, n_used, xs, wgu, bgu, wdn, bdn)


def _combine_body(n_a, tm, dest_ref, ys_ref, route_ref, xm_ref, gfin_ref, ya_ref, yb_ref, buf_ref, sem):
    i = pl.program_id(0)

    def row_copy(r, k):
        d = dest_ref[r * TOP_K + k]
        return pltpu.make_async_copy(ys_ref.at[pl.ds(d, 1), :], buf_ref.at[k, pl.ds(r, 1), :], sem)

    def issue(r, c):
        for k in range(TOP_K):
            row_copy(r, k).start(priority=k % 2)
        return c

    def drain(r, c):
        for k in range(TOP_K):
            row_copy(r, k).wait()
        return c

    lax.fori_loop(0, tm, issue, 0)
    lax.fori_loop(0, tm, drain, 0)
    route = route_ref[...]
    y = jnp.zeros(xm_ref.shape, F32)
    for k in range(TOP_K):
        y = y + buf_ref[k] * route[:, ROUTE_GATE + k:ROUTE_GATE + k + 1]
    out = _rms(xm_ref[...] + y, gfin_ref[...])

    @pl.when(i < n_a)
    def _():
        ya_ref[...] = out

    @pl.when(i >= n_a)
    def _():
        yb_ref[...] = out


def _combine(dest_flat, ys, route, xm, gfin, n_a, tm):
    n, d_model = xm.shape
    n_t = n // tm
    n_b = n_t - n_a
    return pl.pallas_call(
        functools.partial(_combine_body, n_a, tm),
        grid=(n_t,),
        in_specs=[pl.BlockSpec((tm * TOP_K,), lambda i: (i,), memory_space=pltpu.SMEM),
                  pl.BlockSpec(memory_space=pl.ANY),
                  pl.BlockSpec((tm, LANES), lambda i: (i, 0)),
                  pl.BlockSpec((tm, d_model), lambda i: (i, 0)),
                  pl.BlockSpec((1, d_model), lambda i: (0, 0))],
        out_specs=(pl.BlockSpec((tm, d_model), lambda i: (jnp.minimum(i, n_a - 1), 0)),
                   pl.BlockSpec((tm, d_model), lambda i: (jnp.maximum(i - n_a, 0), 0))),
        out_shape=(jax.ShapeDtypeStruct((n_a * tm, d_model), F32),
                   jax.ShapeDtypeStruct((n_b * tm, d_model), F32)),
        scratch_shapes=[pltpu.VMEM((TOP_K, tm, d_model), F32), pltpu.SemaphoreType.DMA(())],
        compiler_params=_cparams(1), name="moe_combine",
    )(dest_flat, ys, route, xm, gfin)


def _prep_in_weights(w_in, b_forget, d_model):
    d_conv = d_model // 2
    s0, s1, s2 = ATT_WIDTH, 2 * ATT_WIDTH, 3 * ATT_WIDTH
    s3 = s2 + N_HEADS
    s4 = s3 + 2 * d_conv
    wf = w_in[:, s2:s3]
    w_all = jnp.concatenate([w_in[:, :s2], w_in[:, s3:s4], w_in[:, s4:]], axis=1).astype(BF16)
    wft = jnp.pad(wf.T, ((0, HEAD_ROWS - N_HEADS), (0, 0))).astype(BF16)
    bfc = jnp.broadcast_to(jnp.pad(b_forget, (0, HEAD_ROWS - N_HEADS))[:, None], (HEAD_ROWS, LANES))
    return w_all, wft, bfc


def _tri_matrices(tm, group):
    i = jnp.arange(tm)
    upper = i[:, None] <= i[None, :]
    same = (i[:, None] // group) == (i[None, :] // group)
    return jnp.stack([upper, jnp.logical_and(upper, same)]).astype(BF16)


def _forward(x_prompt, x_sample, cache_k, cache_v, cache_logf, state_conv, page_table, meta_tokens,
             g_mix, w_in, b_forget, w_dw, b_dw, ln_g, ln_b, w_pw2, w_att_o, w_out, g_ffn,
             w_router, b_router, w_gate_up, b_gate_up, w_down, b_down, g_final,
             tm, tq, pps, t_page, sb, t_route, t_disp, te, t_comb):
    n_batch, seq, d_model = x_prompt.shape
    n_seq, n_new, _ = x_sample.shape
    n_pages = page_table.shape[1]
    n_phys = cache_k.shape[1]
    d_conv = d_model // 2
    n_p = n_batch * seq
    n_s = n_seq * n_new
    n_a = n_p // tm
    layer = 0

    xa = x_prompt.reshape(n_p, d_model)
    xb = x_sample.reshape(n_s, d_model)
    gmix = g_mix[layer][None, :]
    w_all, wft, bfc = _prep_in_weights(w_in[layer], b_forget[layer], d_model)

    meta = meta_tokens.astype(F32)
    (_, k_m, v_m, kb_m, vb_m, lft_m, u_m, _) = _in_project(meta, meta, 1, 0, N_META, gmix, w_all, wft, bfc)
    (q, k_p, k_s, v_p, v_s, kb, vb, lft, u_p, u_s, g) = _in_project(
        xa, xb, n_a, n_s // tm, tm, gmix, w_all, wft, bfc)

    tri_m = _tri_matrices(N_META, N_META)
    ct_m = _cumsum_tokens(lft_m, tri_m, jnp.zeros((HEAD_ROWS, LANES), F32), 1, 1, N_META)
    init = jnp.broadcast_to(ct_m[:, N_META - 1:N_META], (HEAD_ROWS, LANES))
    ct = _cumsum_tokens(lft, _tri_matrices(tm, n_new), init, n_a, seq // tm, tm)

    o_p = _prompt_attention(q, kb, vb, ct, kb_m, vb_m, ct_m, n_batch, seq, tq)

    pt_flat = page_table.reshape(-1).astype(I32)
    lf_pages_t = jnp.swapaxes(cache_logf[layer], 1, 2)
    cache_kt = jnp.transpose(cache_k[layer], (0, 2, 3, 1))
    cache_vt = jnp.transpose(cache_v[layer], (0, 2, 3, 1))
    page_cw = _page_cumsum(lf_pages_t, t_page)
    n_rows = N_HEADS * n_new
    cn = jnp.transpose(ct[:N_HEADS, n_p:].reshape(N_HEADS, n_seq, n_new), (1, 0, 2))
    cn_col = cn.reshape(n_seq, n_rows, 1)
    cn_rows = jnp.broadcast_to(cn[:, :, None, :], (n_seq, N_HEADS, n_new, n_new)).reshape(n_seq, n_rows, n_new)
    cn_rows = jnp.pad(cn_rows, ((0, 0), (0, 0), (0, HEAD_ROWS - n_new)))
    q_s = q[n_p:].astype(F32).reshape(n_seq, n_new, ATT_WIDTH)
    o_s = _sample_attention(pt_flat, cache_kt, cache_vt, page_cw, q_s,
                            k_s.reshape(n_seq, n_new, ATT_WIDTH), v_s.reshape(n_seq, n_new, ATT_WIDTH),
                            cn_rows, cn_col, n_seq, n_pages, n_new, pps)
    o_s = o_s.reshape(n_s, ATT_WIDTH).astype(BF16)

    wdw = jnp.pad(w_dw[layer], ((0, 1), (0, 0)))
    bdw, lng, lnb = b_dw[layer][None, :], ln_g[layer][None, :], ln_b[layer][None, :]
    state0 = jnp.concatenate([jnp.zeros((HALO - N_META, d_conv), F32), u_m], axis=0)
    cs_p = _conv_prompt(u_p, state0, wdw, bdw, lng, lnb, n_batch, seq, tm)
    cs_s, conv_sample = _conv_sample(state_conv[layer], u_s.reshape(n_seq, n_new, d_conv),
                                     wdw, bdw, lng, lnb, sb)
    cs_s = cs_s.reshape(n_s, d_conv)

    wr = jnp.pad(w_router[layer], ((0, 0), (0, LANES - N_EXPERTS)))
    wr1 = wr.astype(BF16)
    wr2 = (wr - wr1.astype(F32)).astype(BF16)
    wr12 = jnp.concatenate([wr1, wr2], axis=1)
    br = jnp.pad(b_router[layer], (0, LANES - N_EXPERTS), constant_values=NEG_BIG)[None, :]
    xm, hf, logits = _merge(xa, xb, cs_p, cs_s, o_p, o_s, g,
                            w_pw2[layer].astype(BF16), w_att_o[layer].astype(BF16), w_out[layer].astype(BF16),
                            g_ffn[layer][None, :], wr12, wr1, br, n_a, tm)

    n_all = n_p + n_s
    route, counts = _route(logits, t_route)
    cnt = counts[0, :N_EXPERTS].astype(I32)
    padded = (cnt + te - 1) // te * te
    pend = jnp.cumsum(padded)
    pstart = pend - padded
    idx = route[:, ROUTE_IDX:ROUTE_IDX + TOP_K].astype(I32)
    rank = route[:, ROUTE_RANK:ROUTE_RANK + TOP_K].astype(I32)
    dest = (pstart[idx] + rank).reshape(-1).astype(I32)
    n_blocks = -(-(n_all * TOP_K) // te) + N_EXPERTS
    n_used = (pend[-1] // te).astype(I32).reshape(1)
    blk = jnp.minimum(jnp.arange(n_blocks, dtype=I32), n_used[0] - 1) * te
    block_expert = jnp.minimum(jnp.sum(pend[None, :] <= blk[:, None], axis=1), N_EXPERTS - 1).astype(I32)
    pend0 = jnp.concatenate([jnp.zeros((1,), I32), pend.astype(I32)])
    xs = _dispatch(pend0, dest, hf, n_blocks * te, t_disp, te)
    ys = _experts(block_expert, n_used, xs, w_gate_up[layer], b_gate_up[layer][:, None, :],
                  w_down[layer], b_down[layer][:, None, :], te)
    y_p, y_s = _combine(dest, ys, route, xm, g_final[None, :], n_p // t_comb, t_comb)

    def with_meta(real, m):
        real = real.reshape((n_batch, seq) + real.shape[1:])
        m = jnp.broadcast_to(m[None], (n_batch,) + m.shape)
        return jnp.concatenate([m, real], axis=1)

    hd = (N_HEADS, HEAD_DIM)
    k_prompt = with_meta(k_p, k_m).reshape((1, n_batch, seq + N_META) + hd)
    v_prompt = with_meta(v_p, v_m).reshape((1, n_batch, seq + N_META) + hd)
    lf_real = jnp.transpose(lft[:N_HEADS, :n_p].reshape(N_HEADS, n_batch, seq), (1, 0, 2))
    lf_meta = jnp.broadcast_to(lft_m[None, :N_HEADS, :], (n_batch, N_HEADS, N_META))
    logf_prompt = jnp.swapaxes(jnp.concatenate([lf_meta, lf_real], axis=2), 1, 2)[None]
    n_st = CONV_WIDTH - 1
    conv_prompt = u_p.reshape(n_batch, seq, d_conv)[:, seq - n_st:][None]
    k_sample = k_s.reshape((1, n_seq, n_new) + hd)
    v_sample = v_s.reshape((1, n_seq, n_new) + hd)
    logf_sample = jnp.transpose(lft[:N_HEADS, n_p:].reshape(N_HEADS, n_seq, n_new), (1, 2, 0))[None]
    return (y_p.reshape(n_batch, seq, d_model), y_s.reshape(n_seq, n_new, d_model),
            k_prompt, v_prompt, logf_prompt, conv_prompt, k_sample, v_sample, logf_sample, conv_sample[None])


def kernel(x_prompt, x_sample, cache_k, cache_v, cache_logf, state_conv, page_table, meta_tokens,
           g_mix, w_in, b_forget, w_dw, b_dw, ln_g, ln_b, w_pw2, w_att_o, w_out, g_ffn,
           w_router, b_router, w_gate_up, b_gate_up, w_down, b_down, g_final):
    return _forward(x_prompt, x_sample, cache_k, cache_v, cache_logf, state_conv, page_table, meta_tokens,
                    g_mix, w_in, b_forget, w_dw, b_dw, ln_g, ln_b, w_pw2, w_att_o, w_out, g_ffn,
                    w_router, b_router, w_gate_up, b_gate_up, w_down, b_down, g_final,
                    tm=512, tq=512, pps=16, t_page=256, sb=16, t_route=512, t_disp=256, te=512, t_comb=256)
```

```python
import functools

import jax
import jax.numpy as jnp
from jax import lax
from jax.experimental import pallas as pl
from jax.experimental.pallas import tpu as pltpu

F32 = jnp.float32
BF16 = jnp.bfloat16
I32 = jnp.int32

N_META = 16
N_HEADS = 8
HEAD_DIM = 64
ATT_WIDTH = N_HEADS * HEAD_DIM
CONV_WIDTH = 31
N_EXPERTS = 32
TOP_K = 4
SWIGLU_LIMIT = 7.0
SWIGLU_ALPHA = 1.702
NORM_EPS = 1e-5
SCALE = HEAD_DIM ** -0.5
PAGE_SIZE = 128

LANES = 128
SUBLANES = 8
HEAD_ROWS = 16
NEG_BIG = -1e30
VMEM_LIMIT = 56 * 1024 * 1024


def _cparams(n_axes):
    return pltpu.CompilerParams(dimension_semantics=("arbitrary",) * n_axes,
                                vmem_limit_bytes=VMEM_LIMIT)


def _dot(a, b):
    return jnp.dot(a, b, preferred_element_type=F32)


def _dot_nt(a, b):
    return lax.dot_general(a, b, (((1,), (1,)), ((), ())), preferred_element_type=F32)


def _split3(x):
    a = x.astype(BF16)
    r = x - a.astype(F32)
    b = r.astype(BF16)
    c = (r - b.astype(F32)).astype(BF16)
    return a, b, c


def _dot_exact_rhs01(x, m):
    a, b, c = _split3(x)
    return _dot(a, m) + _dot(b, m) + _dot(c, m)


def _dot_exact_lhs01(m, x):
    a, b, c = _split3(x)
    return _dot(m, a) + _dot(m, b) + _dot(m, c)


def _log_sigmoid(x):
    return jnp.minimum(x, 0.0) - jnp.log1p(jnp.exp(-jnp.abs(x)))


def _rms(x, g):
    ms = jnp.mean(x * x, axis=-1, keepdims=True)
    return x * lax.rsqrt(ms + NORM_EPS) * g


def _inproj_body(n_a, n_b, d_att, d_conv, d_model,
                 xa_ref, xb_ref, gmix_ref, w_ref, wft_ref, bfc_ref, *out_refs):
    if n_b:
        q_ref, ka_ref, kb2_ref, va_ref, vb2_ref, kb_ref, vb_ref, lft_ref, ua_ref, ub2_ref, g_ref = out_refs
    else:
        q_ref, ka_ref, va_ref, kb_ref, vb_ref, lft_ref, ua_ref, g_ref = out_refs
        kb2_ref = vb2_ref = ub2_ref = None
    i = pl.program_id(0)
    is_a = i < n_a

    def put(ref_a, ref_b, val):
        if ref_b is None:
            ref_a[...] = val
            return

        @pl.when(is_a)
        def _():
            ref_a[...] = val

        @pl.when(jnp.logical_not(is_a))
        def _():
            ref_b[...] = val

    x = jnp.where(is_a, xa_ref[...], xb_ref[...])
    hb = _rms(x, gmix_ref[...]).astype(BF16)
    o = 0
    q_ref[...] = (_dot(hb, w_ref[:, o:o + d_att]) * SCALE).astype(BF16)
    o += d_att
    kk = _dot(hb, w_ref[:, o:o + d_att])
    put(ka_ref, kb2_ref, kk)
    kb_ref[...] = kk.astype(BF16)
    o += d_att
    vv = _dot(hb, w_ref[:, o:o + d_att])
    put(va_ref, vb2_ref, vv)
    vb_ref[...] = vv.astype(BF16)
    o += d_att
    ua = _dot(hb, w_ref[:, o:o + d_conv])
    ub = _dot(hb, w_ref[:, o + d_conv:o + 2 * d_conv])
    put(ua_ref, ub2_ref, ua * jax.nn.sigmoid(ub))
    o += 2 * d_conv
    for c in range(0, 2 * d_model, 512):
        g_ref[:, c:c + 512] = jax.nn.sigmoid(_dot(hb, w_ref[:, o + c:o + c + 512])).astype(BF16)
    lft_ref[...] = _log_sigmoid(_dot_nt(wft_ref[...], hb) + bfc_ref[:, 0:1])


def _in_project(xa, xb, n_a, n_b, tm, gmix, w_all, wft, bfc):
    d_model = xa.shape[1]
    n_t = n_a + n_b
    n = n_t * tm
    d_att, d_conv = ATT_WIDTH, d_model // 2
    wcols = w_all.shape[1]
    row = lambda i: (i, 0)
    ra = lambda i: (jnp.minimum(i, n_a - 1), 0)
    rb = lambda i: (jnp.maximum(i - n_a, 0), 0)
    full = lambda i: (0, 0)

    def split(width):
        out = [(jax.ShapeDtypeStruct((n_a * tm, width), F32), pl.BlockSpec((tm, width), ra))]
        if n_b:
            out.append((jax.ShapeDtypeStruct((n_b * tm, width), F32), pl.BlockSpec((tm, width), rb)))
        return out

    def whole(shape, block, index_map, dtype):
        return [(jax.ShapeDtypeStruct(shape, dtype), pl.BlockSpec(block, index_map))]

    outs = (whole((n, d_att), (tm, d_att), row, BF16)
            + split(d_att) + split(d_att)
            + whole((n, d_att), (tm, d_att), row, BF16)
            + whole((n, d_att), (tm, d_att), row, BF16)
            + whole((HEAD_ROWS, n), (HEAD_ROWS, tm), lambda i: (0, i), F32)
            + split(d_conv)
            + whole((n, 2 * d_model), (tm, 2 * d_model), row, BF16))
    in_specs = [
        pl.BlockSpec((tm, d_model), ra),
        pl.BlockSpec((tm, d_model), rb),
        pl.BlockSpec((1, d_model), full),
        pl.BlockSpec((d_model, wcols), full),
        pl.BlockSpec((HEAD_ROWS, d_model), full),
        pl.BlockSpec((HEAD_ROWS, LANES), full),
    ]
    return pl.pallas_call(
        functools.partial(_inproj_body, n_a, n_b, d_att, d_conv, d_model),
        grid=(n_t,), in_specs=in_specs, out_specs=tuple(s for _, s in outs),
        out_shape=tuple(s for s, _ in outs),
        compiler_params=_cparams(1), name="in_project",
    )(xa, xb, gmix, w_all, wft, bfc)


def _cumsum_body(n_a, seg, lft_ref, tri_ref, init_ref, ct_ref, carry_ref):
    i = pl.program_id(0)
    is_a = i < n_a
    start = jnp.logical_and(is_a, i % seg == 0)
    carry = jnp.where(start, init_ref[...], carry_ref[...])
    carry = jnp.where(is_a, carry, 0.0)
    c = _dot_exact_rhs01(lft_ref[...], tri_ref[0]) + carry[:, 0:1]
    ct_ref[...] = c
    tm = c.shape[1]
    carry_ref[...] = jnp.broadcast_to(c[:, tm - 1:tm], carry_ref.shape)


def _cumsum_tokens(lft, tris, init, n_a, seg, tm):
    n = lft.shape[1]
    return pl.pallas_call(
        functools.partial(_cumsum_body, n_a, seg),
        grid=(n // tm,),
        in_specs=[pl.BlockSpec((HEAD_ROWS, tm), lambda i: (0, i)),
                  pl.BlockSpec((1, tm, tm), lambda i: (jnp.where(i < n_a, 0, 1), 0, 0)),
                  pl.BlockSpec((HEAD_ROWS, LANES), lambda i: (0, 0))],
        out_specs=pl.BlockSpec((HEAD_ROWS, tm), lambda i: (0, i)),
        out_shape=jax.ShapeDtypeStruct((HEAD_ROWS, n), F32),
        scratch_shapes=[pltpu.VMEM((HEAD_ROWS, LANES), F32)],
        compiler_params=_cparams(1), name="decay_cumsum",
    )(lft, tris, init)


def _attn_body(tq, q_ref, k_ref, v_ref, ct_ref, km_ref, vm_ref, ctm_ref, o_ref):
    hp = pl.program_id(1)
    i = pl.program_id(2)
    lane = lax.broadcasted_iota(I32, (1, LANES), 1)
    q2 = q_ref[...]
    km = km_ref[...]
    vm = vm_ref[...]
    rows = lax.broadcasted_iota(I32, (tq, tq), 0)
    cols = lax.broadcasted_iota(I32, (tq, tq), 1)
    outs = []
    for hh in range(2):
        in_head = (lane < HEAD_DIM) if hh == 0 else (lane >= HEAD_DIM)
        qh = jnp.where(in_head, q2, jnp.zeros_like(q2))
        head = 2 * hp + hh
        s = _dot_nt(qh, km) - ctm_ref[pl.ds(head, 1), :]
        m = jnp.max(s, axis=1, keepdims=True)
        p = jnp.exp(s - m)
        l = jnp.sum(p, axis=1, keepdims=True)
        acc = _dot(p.astype(BF16), vm)

        def tile(j, carry, diag):
            m, l, acc = carry
            k0 = pl.multiple_of(j * tq, tq)
            s = _dot_nt(qh, k_ref[pl.ds(k0, tq), :]) - ct_ref[pl.ds(head, 1), pl.ds(k0, tq)]
            if diag:
                s = jnp.where(cols <= rows, s, NEG_BIG)
            mn = jnp.maximum(m, jnp.max(s, axis=1, keepdims=True))
            a = jnp.exp(m - mn)
            p = jnp.exp(s - mn)
            l = a * l + jnp.sum(p, axis=1, keepdims=True)
            acc = a * acc + _dot(p.astype(BF16), v_ref[pl.ds(k0, tq), :])
            return mn, l, acc

        carry = lax.fori_loop(0, i, lambda j, c: tile(j, c, False), (m, l, acc))
        m, l, acc = tile(i, carry, True)
        outs.append(acc / l)
    o_ref[...] = jnp.where(lane < HEAD_DIM, outs[0], outs[1]).astype(o_ref.dtype)


def _prompt_attention(q, kb, vb, ct, kb_m, vb_m, ct_m, n_batch, seq, tq):
    n_hp = ATT_WIDTH // LANES
    n_q = seq // tq
    return pl.pallas_call(
        functools.partial(_attn_body, tq),
        grid=(n_batch, n_hp, n_q),
        in_specs=[pl.BlockSpec((tq, LANES), lambda b, h, i: (b * n_q + i, h)),
                  pl.BlockSpec((seq, LANES), lambda b, h, i: (b, h)),
                  pl.BlockSpec((seq, LANES), lambda b, h, i: (b, h)),
                  pl.BlockSpec((HEAD_ROWS, seq), lambda b, h, i: (0, b)),
                  pl.BlockSpec((N_META, LANES), lambda b, h, i: (0, h)),
                  pl.BlockSpec((N_META, LANES), lambda b, h, i: (0, h)),
                  pl.BlockSpec((HEAD_ROWS, N_META), lambda b, h, i: (0, 0))],
        out_specs=pl.BlockSpec((tq, LANES), lambda b, h, i: (b * n_q + i, h)),
        out_shape=jax.ShapeDtypeStruct((n_batch * seq, ATT_WIDTH), BF16),
        compiler_params=_cparams(3), name="prompt_attention",
    )(q, kb, vb, ct, kb_m, vb_m, ct_m)


def _page_cumsum_body(x_ref, o_ref):
    li = lax.broadcasted_iota(I32, (LANES, LANES), 0)
    lj = lax.broadcasted_iota(I32, (LANES, LANES), 1)
    o_ref[...] = _dot_exact_rhs01(x_ref[...], (li <= lj).astype(BF16))


def _page_cumsum(lf_pages_t, tp):
    n_phys = lf_pages_t.shape[0]
    x = lf_pages_t.reshape(n_phys * N_HEADS, PAGE_SIZE)
    rows = tp * N_HEADS
    out = pl.pallas_call(
        _page_cumsum_body,
        grid=(n_phys // tp,),
        in_specs=[pl.BlockSpec((rows, PAGE_SIZE), lambda i: (i, 0))],
        out_specs=pl.BlockSpec((rows, PAGE_SIZE), lambda i: (i, 0)),
        out_shape=jax.ShapeDtypeStruct(x.shape, F32),
        compiler_params=_cparams(1), name="page_decay_cumsum",
    )(x)
    return out.reshape(n_phys, N_HEADS, PAGE_SIZE)


def _sample_attn_body(pps, n_new, pt_ref, q_ref, kn_ref, vn_ref, cnr_ref, cnc_ref,
                      ck_ref, cv_ref, cw_ref, o_ref, m_ref, l_ref, acc_ref, run_ref, kbuf, vbuf, cbuf, sem):
    g = pl.program_id(1)
    n_g = pl.num_programs(1)
    step = pl.program_id(0) * n_g + g
    n_steps = pl.num_programs(0) * n_g
    slot = step % 2

    def page_copies(s, sl):
        out = []
        for p in range(pps):
            page = pt_ref[s * pps + p]
            out.append(pltpu.make_async_copy(ck_ref.at[page], kbuf.at[sl, p], sem.at[sl, 0]))
            out.append(pltpu.make_async_copy(cv_ref.at[page], vbuf.at[sl, p], sem.at[sl, 1]))
            out.append(pltpu.make_async_copy(cw_ref.at[page], cbuf.at[sl, p], sem.at[sl, 2]))
        return out

    @pl.when(step == 0)
    def _():
        for c in page_copies(step, slot):
            c.start()

    @pl.when(step + 1 < n_steps)
    def _():
        for c in page_copies(step + 1, 1 - slot):
            c.start()

    for c in page_copies(step, slot):
        c.wait()

    n_rows = N_HEADS * n_new
    row = lax.broadcasted_iota(I32, (n_rows, ATT_WIDTH), 0)
    col = lax.broadcasted_iota(I32, (n_rows, ATT_WIDTH), 1)
    own_head = (row // n_new) == (col // HEAD_DIM)
    qbd = jnp.where(own_head, jnp.concatenate([q_ref[0]] * N_HEADS, axis=0), 0.0).astype(BF16)

    @pl.when(g == 0)
    def _():
        m_ref[...] = jnp.full(m_ref.shape, NEG_BIG, F32)
        l_ref[...] = jnp.zeros(l_ref.shape, F32)
        acc_ref[...] = jnp.zeros(acc_ref.shape, F32)
        run_ref[...] = jnp.zeros(run_ref.shape, F32)

    def per_row(x):
        return jnp.broadcast_to(x[:, None, :], (N_HEADS, n_new, PAGE_SIZE)).reshape(n_rows, PAGE_SIZE)

    kt = jnp.concatenate([kbuf[slot, p].reshape(ATT_WIDTH, PAGE_SIZE).astype(BF16) for p in range(pps)], axis=1)
    vt = jnp.concatenate([vbuf[slot, p].reshape(ATT_WIDTH, PAGE_SIZE).astype(BF16) for p in range(pps)], axis=1)
    run = run_ref[...]
    biases = []
    for p in range(pps):
        cw = cbuf[slot, p]
        biases.append(per_row(cw + run))
        run = run + jnp.broadcast_to(cw[:, PAGE_SIZE - 1:PAGE_SIZE], run.shape)
    run_ref[...] = run
    bias = jnp.concatenate(biases, axis=1)
    s = _dot(qbd, kt) - bias
    m_old = m_ref[:, 0:1]
    mn = jnp.maximum(m_old, jnp.max(s, axis=1, keepdims=True))
    a = jnp.exp(m_old - mn)
    p = jnp.exp(s - mn)
    l_new = a * l_ref[:, 0:1] + jnp.sum(p, axis=1, keepdims=True)
    acc_new = a * acc_ref[...] + _dot_nt(p.astype(BF16), vt)
    m_ref[...] = jnp.broadcast_to(mn, m_ref.shape)
    l_ref[...] = jnp.broadcast_to(l_new, l_ref.shape)
    acc_ref[...] = acc_new

    @pl.when(g == n_g - 1)
    def _():
        cn_col = cnc_ref[0][:, 0:1]
        m_past = mn + per_row(run)[:, 0:1] + cn_col
        zpad = jnp.zeros((HEAD_ROWS - n_new, ATT_WIDTH), F32)
        kn = jnp.concatenate([kn_ref[0], zpad], axis=0).astype(BF16)
        vn = jnp.concatenate([vn_ref[0], zpad], axis=0).astype(BF16)
        sn = _dot_nt(qbd, kn) + cn_col - cnr_ref[0]
        r2 = lax.broadcasted_iota(I32, sn.shape, 0) % n_new
        c2 = lax.broadcasted_iota(I32, sn.shape, 1)
        sn = jnp.where(c2 <= r2, sn, NEG_BIG)
        m2 = jnp.maximum(m_past, jnp.max(sn, axis=1, keepdims=True))
        a2 = jnp.exp(m_past - m2)
        pn = jnp.exp(sn - m2)
        l2 = a2 * l_new + jnp.sum(pn, axis=1, keepdims=True)
        acc2 = a2 * acc_new + _dot(pn.astype(BF16), vn)
        o = jnp.where(own_head, acc2 / l2, 0.0)
        o_ref[0] = jnp.sum(o.reshape(N_HEADS, n_new, ATT_WIDTH), axis=0)


def _sample_attention(page_table_flat, cache_kt, cache_vt, page_cw, q_s, k_s, v_s, cn_rows, cn_col,
                      n_seq, n_pages, n_new, pps):
    n_g = n_pages // pps
    n_rows = N_HEADS * n_new

    seq3 = lambda b, g, pt: (b, 0, 0)
    grid_spec = pltpu.PrefetchScalarGridSpec(
        num_scalar_prefetch=1, grid=(n_seq, n_g),
        in_specs=[pl.BlockSpec((1, n_new, ATT_WIDTH), seq3),
                  pl.BlockSpec((1, n_new, ATT_WIDTH), seq3),
                  pl.BlockSpec((1, n_new, ATT_WIDTH), seq3),
                  pl.BlockSpec((1, n_rows, HEAD_ROWS), seq3),
                  pl.BlockSpec((1, n_rows, 1), seq3),
                  pl.BlockSpec(memory_space=pl.ANY), pl.BlockSpec(memory_space=pl.ANY),
                  pl.BlockSpec(memory_space=pl.ANY)],
        out_specs=pl.BlockSpec((1, n_new, ATT_WIDTH), seq3),
        scratch_shapes=[pltpu.VMEM((n_rows, LANES), F32), pltpu.VMEM((n_rows, LANES), F32),
                        pltpu.VMEM((n_rows, ATT_WIDTH), F32),
                        pltpu.VMEM((N_HEADS, PAGE_SIZE), F32),
                        pltpu.VMEM((2, pps, N_HEADS, HEAD_DIM, PAGE_SIZE), F32),
                        pltpu.VMEM((2, pps, N_HEADS, HEAD_DIM, PAGE_SIZE), F32),
                        pltpu.VMEM((2, pps, N_HEADS, PAGE_SIZE), F32),
                        pltpu.SemaphoreType.DMA((2, 3))])
    return pl.pallas_call(
        functools.partial(_sample_attn_body, pps, n_new),
        grid_spec=grid_spec,
        out_shape=jax.ShapeDtypeStruct((n_seq, n_new, ATT_WIDTH), F32),
        compiler_params=_cparams(2), name="sample_attention",
    )(page_table_flat, q_s, k_s, v_s, cn_rows, cn_col, cache_kt, cache_vt, page_cw)


HALO = 32


def _conv_tail(y, bdw, lng, lnb):
    y = y + bdw
    mu = jnp.mean(y, axis=-1, keepdims=True)
    d = y - mu
    var = jnp.mean(d * d, axis=-1, keepdims=True)
    yn = d * lax.rsqrt(var + NORM_EPS) * lng + lnb
    return yn * jax.nn.sigmoid(yn)


def _conv_prompt_body(tm, u_ref, halo_ref, st_ref, wdw_ref, bdw_ref, lng_ref, lnb_ref, cs_ref, ctx_ref, sh_ref):
    i = pl.program_id(1)
    ctx_ref[0:HALO, :] = jnp.where(i == 0, st_ref[...], halo_ref[...])
    ctx_ref[HALO:HALO + tm, :] = u_ref[...]
    n_sh = HALO + tm - SUBLANES
    for b in range(1, SUBLANES):
        sh_ref[b - 1, 0:n_sh, :] = ctx_ref[b:b + n_sh, :]
    off = HALO - (CONV_WIDTH - 1)
    y = jnp.zeros((tm, u_ref.shape[1]), F32)
    for k in range(CONV_WIDTH):
        a, b = divmod(off + k, SUBLANES)
        rows = slice(a * SUBLANES, a * SUBLANES + tm)
        tap = ctx_ref[rows, :] if b == 0 else sh_ref[b - 1, rows, :]
        y = y + tap * wdw_ref[k:k + 1, :]
    cs_ref[...] = _conv_tail(y, bdw_ref[...], lng_ref[...], lnb_ref[...]).astype(cs_ref.dtype)


def _conv_prompt(u, state0, wdw, bdw, lng, lnb, n_batch, seq, tm):
    ch = u.shape[1]
    n_i = seq // tm
    hb = tm // HALO
    full = lambda b, i: (0, 0)
    return pl.pallas_call(
        functools.partial(_conv_prompt_body, tm),
        grid=(n_batch, n_i),
        in_specs=[pl.BlockSpec((tm, ch), lambda b, i: (b * n_i + i, 0)),
                  pl.BlockSpec((HALO, ch), lambda b, i: (jnp.maximum((b * n_i + i) * hb - 1, 0), 0)),
                  pl.BlockSpec((HALO, ch), full),
                  pl.BlockSpec((CONV_WIDTH + 1, ch), full),
                  pl.BlockSpec((1, ch), full), pl.BlockSpec((1, ch), full), pl.BlockSpec((1, ch), full)],
        out_specs=pl.BlockSpec((tm, ch), lambda b, i: (b * n_i + i, 0)),
        out_shape=jax.ShapeDtypeStruct((n_batch * seq, ch), BF16),
        scratch_shapes=[pltpu.VMEM((HALO + tm, ch), F32), pltpu.VMEM((SUBLANES - 1, HALO + tm, ch), F32)],
        compiler_params=_cparams(2), name="conv_prompt",
    )(u, u, state0, wdw, bdw, lng, lnb)


def _conv_sample_body(n_new, st_ref, u_ref, wdw_ref, bdw_ref, lng_ref, lnb_ref, cs_ref, st_out_ref, ctx_ref):
    n_st = CONV_WIDTH - 1
    off = HALO - n_st
    ctx_ref[:, off:HALO, :] = st_ref[...]
    ctx_ref[:, HALO:HALO + n_new, :] = u_ref[...]
    sb, _, ch = u_ref.shape
    y = jnp.zeros((sb, n_new, ch), F32)
    for k in range(CONV_WIDTH):
        y = y + ctx_ref[:, off + k:off + k + n_new, :] * wdw_ref[k:k + 1, :]
    cs_ref[...] = _conv_tail(y, bdw_ref[...], lng_ref[...], lnb_ref[...]).astype(cs_ref.dtype)
    st_out_ref[...] = ctx_ref[:, off + n_new:off + n_new + n_st, :]


def _conv_sample(state, u, wdw, bdw, lng, lnb, sb):
    n_seq, n_new, ch = u.shape
    n_st = CONV_WIDTH - 1
    full = lambda i: (0, 0)
    blk = lambda i: (i, 0, 0)
    return pl.pallas_call(
        functools.partial(_conv_sample_body, n_new),
        grid=(n_seq // sb,),
        in_specs=[pl.BlockSpec((sb, n_st, ch), blk), pl.BlockSpec((sb, n_new, ch), blk),
                  pl.BlockSpec((CONV_WIDTH + 1, ch), full),
                  pl.BlockSpec((1, ch), full), pl.BlockSpec((1, ch), full), pl.BlockSpec((1, ch), full)],
        out_specs=(pl.BlockSpec((sb, n_new, ch), blk), pl.BlockSpec((sb, n_st, ch), blk)),
        out_shape=(jax.ShapeDtypeStruct((n_seq, n_new, ch), BF16),
                   jax.ShapeDtypeStruct((n_seq, n_st, ch), F32)),
        scratch_shapes=[pltpu.VMEM((sb, HALO + n_new, ch), F32)],
        compiler_params=_cparams(1), name="conv_sample",
    )(state, u, wdw, bdw, lng, lnb)


def _merge_body(n_a, d_model, xa_ref, xb_ref, csa_ref, csb_ref, oa_ref, ob_ref, g_ref,
                wpw_ref, wao_ref, wout_ref, gffn_ref, wr12_ref, wr1_ref, br_ref,
                xm_ref, hf_ref, lg_ref):
    i = pl.program_id(0)
    is_a = i < n_a
    x = jnp.where(is_a, xa_ref[...], xb_ref[...])
    cs = jnp.where(is_a, csa_ref[...], csb_ref[...])
    o = jnp.where(is_a, oa_ref[...], ob_ref[...])
    conv = _dot(cs, wpw_ref[...])
    att = _dot(o, wao_ref[...])
    mix = g_ref[:, 0:d_model].astype(F32) * conv + g_ref[:, d_model:2 * d_model].astype(F32) * att
    xm = x + _dot(mix.astype(BF16), wout_ref[...])
    xm_ref[...] = xm
    hf = _rms(xm, gffn_ref[...])
    hf_ref[...] = hf
    h1 = hf.astype(BF16)
    h2 = (hf - h1.astype(F32)).astype(BF16)
    big = _dot(h1, wr12_ref[...])
    lg_ref[...] = big[:, 0:LANES] + big[:, LANES:2 * LANES] + _dot(h2, wr1_ref[...]) + br_ref[...]


def _merge(xa, xb, csa, csb, oa, ob, g, wpw, wao, wout, gffn, wr12, wr1, br, n_a, tm):
    d_model = xa.shape[1]
    n = g.shape[0]
    n_t = n // tm
    row = lambda i: (i, 0)
    ra = lambda i: (jnp.minimum(i, n_a - 1), 0)
    rb = lambda i: (jnp.maximum(i - n_a, 0), 0)
    full = lambda i: (0, 0)
    dc, da = csa.shape[1], oa.shape[1]
    return pl.pallas_call(
        functools.partial(_merge_body, n_a, d_model),
        grid=(n_t,),
        in_specs=[pl.BlockSpec((tm, d_model), ra), pl.BlockSpec((tm, d_model), rb),
                  pl.BlockSpec((tm, dc), ra), pl.BlockSpec((tm, dc), rb),
                  pl.BlockSpec((tm, da), ra), pl.BlockSpec((tm, da), rb),
                  pl.BlockSpec((tm, 2 * d_model), row),
                  pl.BlockSpec((dc, d_model), full), pl.BlockSpec((da, d_model), full),
                  pl.BlockSpec((d_model, d_model), full), pl.BlockSpec((1, d_model), full),
                  pl.BlockSpec((d_model, 2 * LANES), full), pl.BlockSpec((d_model, LANES), full),
                  pl.BlockSpec((1, LANES), full)],
        out_specs=(pl.BlockSpec((tm, d_model), row), pl.BlockSpec((tm, d_model), row),
                   pl.BlockSpec((tm, LANES), row)),
        out_shape=(jax.ShapeDtypeStruct((n, d_model), F32), jax.ShapeDtypeStruct((n, d_model), F32),
                   jax.ShapeDtypeStruct((n, LANES), F32)),
        compiler_params=_cparams(1), name="merge_router",
    )(xa, xb, csa, csb, oa, ob, g, wpw, wao, wout, gffn, wr12, wr1, br)


ROUTE_IDX, ROUTE_RANK, ROUTE_GATE = 0, TOP_K, 2 * TOP_K


def _route_body(lg_ref, route_ref, cnt_ref, carry_ref):
    i = pl.program_id(0)

    @pl.when(i == 0)
    def _():
        carry_ref[...] = jnp.zeros(carry_ref.shape, F32)

    lg = lg_ref[...]
    tm = lg.shape[0]
    lane = lax.broadcasted_iota(I32, lg.shape, 1).astype(F32)
    vals, sels, idxs = [], [], []
    for _ in range(TOP_K):
        mk = jnp.max(lg, axis=1, keepdims=True)
        ik = jnp.min(jnp.where(lg == mk, lane, float(LANES)), axis=1, keepdims=True)
        sel = lane == ik
        vals.append(mk)
        idxs.append(ik)
        sels.append(sel)
        lg = jnp.where(sel, -3e38, lg)
    onehot = sum(s.astype(F32) for s in sels)
    es = [jnp.exp(v - vals[0]) for v in vals]
    den = sum(es)
    ri = lax.broadcasted_iota(I32, (tm, tm), 0)
    rj = lax.broadcasted_iota(I32, (tm, tm), 1)
    before = (rj < ri).astype(BF16)
    carry = carry_ref[0:1, :]
    cum = _dot(before, onehot.astype(BF16)) + carry
    route = jnp.zeros(lg.shape, F32)
    for k in range(TOP_K):
        rank = jnp.sum(jnp.where(sels[k], cum, 0.0), axis=1, keepdims=True)
        route = jnp.where(lane == ROUTE_IDX + k, idxs[k], route)
        route = jnp.where(lane == ROUTE_RANK + k, rank, route)
        route = jnp.where(lane == ROUTE_GATE + k, es[k] / den, route)
    route_ref[...] = route
    total = carry + jnp.sum(onehot, axis=0, keepdims=True)
    carry_ref[...] = jnp.broadcast_to(total, carry_ref.shape)
    cnt_ref[...] = jnp.broadcast_to(total, cnt_ref.shape)


def _route(logits, tm):
    n = logits.shape[0]
    return pl.pallas_call(
        _route_body,
        grid=(n // tm,),
        in_specs=[pl.BlockSpec((tm, LANES), lambda i: (i, 0))],
        out_specs=(pl.BlockSpec((tm, LANES), lambda i: (i, 0)), pl.BlockSpec((8, LANES), lambda i: (0, 0))),
        out_shape=(jax.ShapeDtypeStruct((n, LANES), F32), jax.ShapeDtypeStruct((8, LANES), F32)),
        scratch_shapes=[pltpu.VMEM((8, LANES), F32)],
        compiler_params=_cparams(1), name="route_topk",
    )(logits)


def _dispatch_body(tm, te, n_blocks, pend_ref, dest_ref, hf_ref, xs_ref, zero_ref, sem):
    i = pl.program_id(0)

    @pl.when(i == 0)
    def _():
        zero_ref[...] = jnp.zeros(zero_ref.shape, F32)

        def zero_block(start):
            return pltpu.make_async_copy(zero_ref, xs_ref.at[pl.ds(pl.multiple_of(start, te), te), :], sem)

        def tails(wait):
            def body(e, c):
                @pl.when(pend_ref[e + 1] > pend_ref[e])
                def _():
                    cp = zero_block(pend_ref[e + 1] - te)
                    cp.wait() if wait else cp.start()
                return c
            lax.fori_loop(0, N_EXPERTS, body, 0)

        def unused(wait):
            def body(bk, c):
                cp = zero_block(bk * te)
                cp.wait() if wait else cp.start()
                return c
            lax.fori_loop(lax.div(pend_ref[N_EXPERTS], te), n_blocks, body, 0)

        tails(False)
        unused(False)
        tails(True)
        unused(True)

    def row_copy(r, k):
        return pltpu.make_async_copy(hf_ref.at[pl.ds(r, 1), :], xs_ref.at[pl.ds(dest_ref[r * TOP_K + k], 1), :], sem)

    def rows(wait):
        def body(r, c):
            for k in range(TOP_K):
                cp = row_copy(r, k)
                cp.wait() if wait else cp.start()
            return c
        lax.fori_loop(0, tm, body, 0, unroll=8)

    rows(False)
    rows(True)


def _dispatch(pend0, dest_flat, hf, n_rows, tm, te):
    n, d_model = hf.shape
    grid_spec = pltpu.PrefetchScalarGridSpec(
        num_scalar_prefetch=1, grid=(n // tm,),
        in_specs=[pl.BlockSpec((tm * TOP_K,), lambda i, pe: (i,), memory_space=pltpu.SMEM),
                  pl.BlockSpec((tm, d_model), lambda i, pe: (i, 0))],
        out_specs=pl.BlockSpec(memory_space=pl.ANY),
        scratch_shapes=[pltpu.VMEM((te, d_model), F32), pltpu.SemaphoreType.DMA(())])
    return pl.pallas_call(
        functools.partial(_dispatch_body, tm, te, n_rows // te),
        grid_spec=grid_spec,
        out_shape=jax.ShapeDtypeStruct((n_rows, d_model), F32),
        compiler_params=_cparams(1), name="moe_dispatch",
    )(pend0, dest_flat, hf)


def _expert_body(d_exp, be_ref, nu_ref, x_ref, wgu_ref, bgu_ref, wdn_ref, bdn_ref, y_ref, wgu_b, wdn_b):
    i = pl.program_id(0)
    used = i < nu_ref[0]
    prev = be_ref[jnp.maximum(i - 1, 0)]
    fresh = jnp.logical_and(used, jnp.logical_or(i == 0, be_ref[i] != prev))

    @pl.when(fresh)
    def _():
        wgu_b[...] = wgu_ref[0].astype(BF16)
        wdn_b[...] = wdn_ref[0].astype(BF16)

    @pl.when(used)
    def _():
        xb = x_ref[...].astype(BF16)
        gu = _dot(xb, wgu_b[...]) + bgu_ref[0]
        gate = jnp.minimum(gu[:, 0:d_exp], SWIGLU_LIMIT)
        lin = jnp.clip(gu[:, d_exp:2 * d_exp], -SWIGLU_LIMIT, SWIGLU_LIMIT)
        act = gate * jax.nn.sigmoid(SWIGLU_ALPHA * gate) * (lin + 1.0)
        y_ref[...] = _dot(act.astype(BF16), wdn_b[...]) + bdn_ref[0]

    @pl.when(jnp.logical_not(used))
    def _():
        y_ref[...] = jnp.zeros(y_ref.shape, F32)


def _experts(block_expert, n_used, xs, wgu, bgu, wdn, bdn, te):
    n_rows, d_model = xs.shape
    d_exp = wdn.shape[1]
    n_blocks = n_rows // te
    rowmap = lambda i, be, nu: (jnp.minimum(i, nu[0] - 1), 0)
    emap = lambda i, be, nu: (be[i], 0, 0)
    grid_spec = pltpu.PrefetchScalarGridSpec(
        num_scalar_prefetch=2, grid=(n_blocks,),
        in_specs=[pl.BlockSpec((te, d_model), rowmap),
                  pl.BlockSpec((1, d_model, 2 * d_exp), emap),
                  pl.BlockSpec((1, 1, 2 * d_exp), emap),
                  pl.BlockSpec((1, d_exp, d_model), emap),
                  pl.BlockSpec((1, 1, d_model), emap)],
        out_specs=pl.BlockSpec((te, d_model), lambda i, be, nu: (i, 0)),
        scratch_shapes=[pltpu.VMEM((d_model, 2 * d_exp), BF16), pltpu.VMEM((d_exp, d_model), BF16)])
    return pl.pallas_call(
        functools.partial(_expert_body, d_exp),
        grid_spec=grid_spec,
        out_shape=jax.ShapeDtypeStruct((n_rows, d_model), F32),
        compiler_params=_cparams(1), name="moe_experts",
    )(block_expert, n_used, xs, wgu, bgu, wdn, bdn)


def _combine_body(n_a, tm, dest_ref, ys_ref, route_ref, xm_ref, gfin_ref, ya_ref, yb_ref, buf_ref, sem):
    i = pl.program_id(0)

    def row_copy(r, k):
        d = dest_ref[r * TOP_K + k]
        return pltpu.make_async_copy(ys_ref.at[pl.ds(d, 1), :], buf_ref.at[k, pl.ds(r, 1), :], sem)

    def issue(r, c):
        for k in range(TOP_K):
            row_copy(r, k).start(priority=k % 2)
        return c

    def drain(r, c):
        for k in range(TOP_K):
            row_copy(r, k).wait()
        return c

    lax.fori_loop(0, tm, issue, 0)
    lax.fori_loop(0, tm, drain, 0)
    route = route_ref[...]
    y = jnp.zeros(xm_ref.shape, F32)
    for k in range(TOP_K):
        y = y + buf_ref[k] * route[:, ROUTE_GATE + k:ROUTE_GATE + k + 1]
    out = _rms(xm_ref[...] + y, gfin_ref[...])

    @pl.when(i < n_a)
    def _():
        ya_ref[...] = out

    @pl.when(i >= n_a)
    def _():
        yb_ref[...] = out


def _combine(dest_flat, ys, route, xm, gfin, n_a, tm):
    n, d_model = xm.shape
    n_t = n // tm
    n_b = n_t - n_a
    return pl.pallas_call(
        functools.partial(_combine_body, n_a, tm),
        grid=(n_t,),
        in_specs=[pl.BlockSpec((tm * TOP_K,), lambda i: (i,), memory_space=pltpu.SMEM),
                  pl.BlockSpec(memory_space=pl.ANY),
                  pl.BlockSpec((tm, LANES), lambda i: (i, 0)),
                  pl.BlockSpec((tm, d_model), lambda i: (i, 0)),
                  pl.BlockSpec((1, d_model), lambda i: (0, 0))],
        out_specs=(pl.BlockSpec((tm, d_model), lambda i: (jnp.minimum(i, n_a - 1), 0)),
                   pl.BlockSpec((tm, d_model), lambda i: (jnp.maximum(i - n_a, 0), 0))),
        out_shape=(jax.ShapeDtypeStruct((n_a * tm, d_model), F32),
                   jax.ShapeDtypeStruct((n_b * tm, d_model), F32)),
        scratch_shapes=[pltpu.VMEM((TOP_K, tm, d_model), F32), pltpu.SemaphoreType.DMA(())],
        compiler_params=_cparams(1), name="moe_combine",
    )(dest_flat, ys, route, xm, gfin)


def _prep_in_weights(w_in, b_forget, d_model):
    d_conv = d_model // 2
    s0, s1, s2 = ATT_WIDTH, 2 * ATT_WIDTH, 3 * ATT_WIDTH
    s3 = s2 + N_HEADS
    s4 = s3 + 2 * d_conv
    wf = w_in[:, s2:s3]
    w_all = jnp.concatenate([w_in[:, :s2], w_in[:, s3:s4], w_in[:, s4:]], axis=1).astype(BF16)
    wft = jnp.pad(wf.T, ((0, HEAD_ROWS - N_HEADS), (0, 0))).astype(BF16)
    bfc = jnp.broadcast_to(jnp.pad(b_forget, (0, HEAD_ROWS - N_HEADS))[:, None], (HEAD_ROWS, LANES))
    return w_all, wft, bfc


def _tri_matrices(tm, group):
    i = jnp.arange(tm)
    upper = i[:, None] <= i[None, :]
    same = (i[:, None] // group) == (i[None, :] // group)
    return jnp.stack([upper, jnp.logical_and(upper, same)]).astype(BF16)


def _forward(x_prompt, x_sample, cache_k, cache_v, cache_logf, state_conv, page_table, meta_tokens,
             g_mix, w_in, b_forget, w_dw, b_dw, ln_g, ln_b, w_pw2, w_att_o, w_out, g_ffn,
             w_router, b_router, w_gate_up, b_gate_up, w_down, b_down, g_final,
             tm, tq, pps, t_page, sb, t_route, t_disp, te, t_comb):
    n_batch, seq, d_model = x_prompt.shape
    n_seq, n_new, _ = x_sample.shape
    n_pages = page_table.shape[1]
    n_phys = cache_k.shape[1]
    d_conv = d_model // 2
    n_p = n_batch * seq
    n_s = n_seq * n_new
    n_a = n_p // tm
    layer = 0

    xa = x_prompt.reshape(n_p, d_model)
    xb = x_sample.reshape(n_s, d_model)
    gmix = g_mix[layer][None, :]
    w_all, wft, bfc = _prep_in_weights(w_in[layer], b_forget[layer], d_model)

    meta = meta_tokens.astype(F32)
    (_, k_m, v_m, kb_m, vb_m, lft_m, u_m, _) = _in_project(meta, meta, 1, 0, N_META, gmix, w_all, wft, bfc)
    (q, k_p, k_s, v_p, v_s, kb, vb, lft, u_p, u_s, g) = _in_project(
        xa, xb, n_a, n_s // tm, tm, gmix, w_all, wft, bfc)

    tri_m = _tri_matrices(N_META, N_META)
    ct_m = _cumsum_tokens(lft_m, tri_m, jnp.zeros((HEAD_ROWS, LANES), F32), 1, 1, N_META)
    init = jnp.broadcast_to(ct_m[:, N_META - 1:N_META], (HEAD_ROWS, LANES))
    ct = _cumsum_tokens(lft, _tri_matrices(tm, n_new), init, n_a, seq // tm, tm)

    o_p = _prompt_attention(q, kb, vb, ct, kb_m, vb_m, ct_m, n_batch, seq, tq)

    pt_flat = page_table.reshape(-1).astype(I32)
    lf_pages_t = jnp.swapaxes(cache_logf[layer], 1, 2)
    cache_kt = jnp.transpose(cache_k[layer], (0, 2, 3, 1))
    cache_vt = jnp.transpose(cache_v[layer], (0, 2, 3, 1))
    page_cw = _page_cumsum(lf_pages_t, t_page)
    n_rows = N_HEADS * n_new
    cn = jnp.transpose(ct[:N_HEADS, n_p:].reshape(N_HEADS, n_seq, n_new), (1, 0, 2))
    cn_col = cn.reshape(n_seq, n_rows, 1)
    cn_rows = jnp.broadcast_to(cn[:, :, None, :], (n_seq, N_HEADS, n_new, n_new)).reshape(n_seq, n_rows, n_new)
    cn_rows = jnp.pad(cn_rows, ((0, 0), (0, 0), (0, HEAD_ROWS - n_new)))
    q_s = q[n_p:].astype(F32).reshape(n_seq, n_new, ATT_WIDTH)
    o_s = _sample_attention(pt_flat, cache_kt, cache_vt, page_cw, q_s,
                            k_s.reshape(n_seq, n_new, ATT_WIDTH), v_s.reshape(n_seq, n_new, ATT_WIDTH),
                            cn_rows, cn_col, n_seq, n_pages, n_new, pps)
    o_s = o_s.reshape(n_s, ATT_WIDTH).astype(BF16)

    wdw = jnp.pad(w_dw[layer], ((0, 1), (0, 0)))
    bdw, lng, lnb = b_dw[layer][None, :], ln_g[layer][None, :], ln_b[layer][None, :]
    state0 = jnp.concatenate([jnp.zeros((HALO - N_META, d_conv), F32), u_m], axis=0)
    cs_p = _conv_prompt(u_p, state0, wdw, bdw, lng, lnb, n_batch, seq, tm)
    cs_s, conv_sample = _conv_sample(state_conv[layer], u_s.reshape(n_seq, n_new, d_conv),
                                     wdw, bdw, lng, lnb, sb)
    cs_s = cs_s.reshape(n_s, d_conv)

    wr = jnp.pad(w_router[layer], ((0, 0), (0, LANES - N_EXPERTS)))
    wr1 = wr.astype(BF16)
    wr2 = (wr - wr1.astype(F32)).astype(BF16)
    wr12 = jnp.concatenate([wr1, wr2], axis=1)
    br = jnp.pad(b_router[layer], (0, LANES - N_EXPERTS), constant_values=NEG_BIG)[None, :]
    xm, hf, logits = _merge(xa, xb, cs_p, cs_s, o_p, o_s, g,
                            w_pw2[layer].astype(BF16), w_att_o[layer].astype(BF16), w_out[layer].astype(BF16),
                            g_ffn[layer][None, :], wr12, wr1, br, n_a, tm)

    n_all = n_p + n_s
    route, counts = _route(logits, t_route)
    cnt = counts[0, :N_EXPERTS].astype(I32)
    padded = (cnt + te - 1) // te * te
    pend = jnp.cumsum(padded)
    pstart = pend - padded
    idx = route[:, ROUTE_IDX:ROUTE_IDX + TOP_K].astype(I32)
    rank = route[:, ROUTE_RANK:ROUTE_RANK + TOP_K].astype(I32)
    dest = (pstart[idx] + rank).reshape(-1).astype(I32)
    n_blocks = -(-(n_all * TOP_K) // te) + N_EXPERTS
    n_used = (pend[-1] // te).astype(I32).reshape(1)
    blk = jnp.minimum(jnp.arange(n_blocks, dtype=I32), n_used[0] - 1) * te
    block_expert = jnp.minimum(jnp.sum(pend[None, :] <= blk[:, None], axis=1), N_EXPERTS - 1).astype(I32)
    pend0 = jnp.concatenate([jnp.zeros((1,), I32), pend.astype(I32)])
    xs = _dispatch(pend0, dest, hf, n_blocks * te, t_disp, te)
    ys = _experts(block_expert, n_used, xs, w_gate_up[layer], b_gate_up[layer][:, None, :],
                  w_down[layer], b_down[layer][:, None, :], te)
    y_p, y_s = _combine(dest, ys, route, xm, g_final[None, :], n_p // t_comb, t_comb)

    def with_meta(real, m):
        real = real.reshape((n_batch, seq) + real.shape[1:])
        m = jnp.broadcast_to(m[None], (n_batch,) + m.shape)
        return jnp.concatenate([m, real], axis=1)

    hd = (N_HEADS, HEAD_DIM)
    k_prompt = with_meta(k_p, k_m).reshape((1, n_batch, seq + N_META) + hd)
    v_prompt = with_meta(v_p, v_m).reshape((1, n_batch, seq + N_META) + hd)
    lf_real = jnp.transpose(lft[:N_HEADS, :n_p].reshape(N_HEADS, n_batch, seq), (1, 0, 2))
    lf_meta = jnp.broadcast_to(lft_m[None, :N_HEADS, :], (n_batch, N_HEADS, N_META))
    logf_prompt = jnp.swapaxes(jnp.concatenate([lf_meta, lf_real], axis=2), 1, 2)[None]
    n_st = CONV_WIDTH - 1
    conv_prompt = u_p.reshape(n_batch, seq, d_conv)[:, seq - n_st:][None]
    k_sample = k_s.reshape((1, n_seq, n_new) + hd)
    v_sample = v_s.reshape((1, n_seq, n_new) + hd)
    logf_sample = jnp.transpose(lft[:N_HEADS, n_p:].reshape(N_HEADS, n_seq, n_new), (1, 2, 0))[None]
    return (y_p.reshape(n_batch, seq, d_model), y_s.reshape(n_seq, n_new, d_model),
            k_prompt, v_prompt, logf_prompt, conv_prompt, k_sample, v_sample, logf_sample, conv_sample[None])


def kernel(x_prompt, x_sample, cache_k, cache_v, cache_logf, state_conv, page_table, meta_tokens,
           g_mix, w_in, b_forget, w_dw, b_dw, ln_g, ln_b, w_pw2, w_att_o, w_out, g_ffn,
           w_router, b_router, w_gate_up, b_gate_up, w_down, b_down, g_final):
    return _forward(x_prompt, x_sample, cache_k, cache_v, cache_logf, state_conv, page_table, meta_tokens,
                    g_mix, w_in, b_forget, w_dw, b_dw, ln_g, ln_b, w_pw2, w_att_o, w_out, g_ffn,
                    w_router, b_router, w_gate_up, b_gate_up, w_down, b_down, g_final,
                    tm=512, tq=512, pps=32, t_page=256, sb=16, t_route=512, t_disp=256, te=512, t_comb=256)
```
